```python
import jax, jax.numpy as jnp
from jax import lax
import numpy as np

D_MODEL = 2048
BATCH = 8
SEQ = 2048
DEPTH = 2

GRID_W = 64
EPS = 1e-6
CONV_CH = 512
CONV_WIDTH = 31
CONV_PAD = (CONV_WIDTH - 1) // 2
MLA_HEADS = 8
MLA_Q_RANK = 512
MLA_KV_RANK = 256
MLA_NOPE = 128
MLA_ROPE = 64
MLA_V = 128
ROPE_THETA = 10000.0
Q_BLOCK = 128
NA_HEADS = 8
NA_HEAD_DIM = 64
NA_ROWS_MAX = 8
NA_COLS = 16
MIX_CONV = CONV_CH
MIX_MLA = MLA_HEADS * MLA_V
MIX_NA = NA_HEADS * NA_HEAD_DIM
MIX_WIDTH = MIX_CONV + MIX_MLA + MIX_NA
IN_SPLIT_SIZES = (2 * CONV_CH, MLA_Q_RANK, MLA_KV_RANK, MLA_ROPE, MIX_NA, MIX_NA, MIX_NA)
IN_COLS = int(sum(IN_SPLIT_SIZES))
IN_SPLIT_POINTS = tuple(int(v) for v in np.cumsum(IN_SPLIT_SIZES)[:-1])
D_FF = 4 * D_MODEL

kernel_name = "hybrid_conv_mla_natten_encoder"


def rms_norm(x, g):
    xf = x.astype(jnp.float32)
    y = xf * lax.rsqrt(jnp.mean(xf * xf, axis=-1, keepdims=True) + EPS)
    return (y * g.astype(jnp.float32)).astype(x.dtype)


def layer_norm(x, g, b):
    xf = x.astype(jnp.float32)
    mu = jnp.mean(xf, axis=-1, keepdims=True)
    xc = xf - mu
    var = jnp.mean(xc * xc, axis=-1, keepdims=True)
    y = xc * lax.rsqrt(var + EPS) * g.astype(jnp.float32) + b.astype(jnp.float32)
    return y.astype(x.dtype)


def rope_tables(seq_len):
    pos = jnp.arange(seq_len, dtype=jnp.float32)
    inv_freq = 1.0 / (ROPE_THETA ** (jnp.arange(0, MLA_ROPE, 2, dtype=jnp.float32) / MLA_ROPE))
    ang = pos[:, None] * inv_freq[None, :]
    return jnp.cos(ang), jnp.sin(ang)


def apply_rope(x, cos, sin):
    cos = cos.astype(x.dtype)
    sin = sin.astype(x.dtype)
    x1, x2 = jnp.split(x, 2, axis=-1)
    return jnp.concatenate([x1 * cos - x2 * sin, x2 * cos + x1 * sin], axis=-1)


def conformer_conv(u, conv_w, conv_b, ln_g, ln_b):
    a, gate = jnp.split(u, 2, axis=-1)
    h = a * jax.nn.sigmoid(gate)
    h = lax.conv_general_dilated(
        h, conv_w[:, None, :].astype(h.dtype), window_strides=(1,),
        padding=[(CONV_PAD, CONV_PAD)], dimension_numbers=("NWC", "WIO", "NWC"),
        feature_group_count=CONV_CH)
    h = h + conv_b
    h = layer_norm(h, ln_g, ln_b)
    return jax.nn.silu(h)


def mla_attention(c_q, c_kv, k_rope_in, g_q_a, w_uq, g_kv_a, w_ukv, cos, sin):
    B, S, _ = c_q.shape
    q = (rms_norm(c_q, g_q_a) @ w_uq).reshape(B, S, MLA_HEADS, MLA_NOPE + MLA_ROPE)
    q_nope, q_pe = q[..., :MLA_NOPE], q[..., MLA_NOPE:]
    q_pe = apply_rope(q_pe, cos[:, None, :], sin[:, None, :])
    kv = (rms_norm(c_kv, g_kv_a) @ w_ukv).reshape(B, S, MLA_HEADS, MLA_NOPE + MLA_V)
    k_nope, v = kv[..., :MLA_NOPE], kv[..., MLA_NOPE:]
    k_pe = apply_rope(k_rope_in, cos, sin)
    scale = (MLA_NOPE + MLA_ROPE) ** -0.5
    nb = S // Q_BLOCK
    qn_b = q_nope.reshape(B, nb, Q_BLOCK, MLA_HEADS, MLA_NOPE).transpose(1, 0, 2, 3, 4)
    qp_b = q_pe.reshape(B, nb, Q_BLOCK, MLA_HEADS, MLA_ROPE).transpose(1, 0, 2, 3, 4)

    def block(args):
        qn, qp = args
        s = (jnp.einsum("bqhd,bkhd->bhqk", qn, k_nope, preferred_element_type=jnp.float32)
             + jnp.einsum("bqhd,bkd->bhqk", qp, k_pe, preferred_element_type=jnp.float32)) * scale
        p = jax.nn.softmax(s, axis=-1)
        return jnp.einsum("bhqk,bkhd->bqhd", p.astype(v.dtype), v)

    o = lax.map(block, (qn_b, qp_b))
    return o.transpose(1, 0, 2, 3, 4).reshape(B, S, MIX_MLA)


def neighbourhood_attention(q, k, v, rpb):
    B, S, _ = q.shape
    rows = S // GRID_W
    kr = min(NA_ROWS_MAX, rows)
    kc = NA_COLS
    qg = q.reshape(B, rows, GRID_W, NA_HEADS, NA_HEAD_DIM)
    kg = k.reshape(B, rows, GRID_W, NA_HEADS, NA_HEAD_DIM)
    vg = v.reshape(B, rows, GRID_W, NA_HEADS, NA_HEAD_DIM)
    cols = jnp.arange(GRID_W)
    col_start = jnp.clip(cols - kc // 2, 0, GRID_W - kc)
    col_idx = col_start[:, None] + jnp.arange(kc)[None, :]
    dc = col_idx - cols[:, None] + (NA_COLS - 1)
    scale = NA_HEAD_DIM ** -0.5

    def row_block(r):
        r0 = jnp.clip(r - kr // 2, 0, rows - kr)
        k_strip = lax.dynamic_slice_in_dim(kg, r0, kr, axis=1)
        v_strip = lax.dynamic_slice_in_dim(vg, r0, kr, axis=1)
        k_win = k_strip[:, :, col_idx]
        v_win = v_strip[:, :, col_idx]
        q_row = lax.dynamic_index_in_dim(qg, r, axis=1, keepdims=False)
        s = jnp.einsum("bwhd,brwkhd->bhwrk", q_row, k_win,
                       preferred_element_type=jnp.float32) * scale
        dr = r0 + jnp.arange(kr) - r + (NA_ROWS_MAX - 1)
        bias = rpb[:, dr[:, None, None], dc[None, :, :]]
        s = s + bias.transpose(0, 2, 1, 3)[None].astype(jnp.float32)
        p = jax.nn.softmax(s.reshape(B, NA_HEADS, GRID_W, kr * kc), axis=-1)
        p = p.reshape(B, NA_HEADS, GRID_W, kr, kc).astype(v.dtype)
        return jnp.einsum("bhwrk,brwkhd->bwhd", p, v_win)

    o = lax.map(row_block, jnp.arange(rows))
    return o.transpose(1, 0, 2, 3, 4).reshape(B, S, MIX_NA)


def setup_inputs(seed: int = 0) -> dict:
    key = jax.random.key(seed)
    ks = jax.random.split(key, 21)
    L = DEPTH

    def nrm(k, shape, scale):
        return jax.random.normal(k, shape, jnp.float32) * scale

    def gain(k, shape):
        return 1.0 + 0.02 * jax.random.normal(k, shape, jnp.float32)

    return {
        "x": nrm(ks[0], (BATCH, SEQ, D_MODEL), 1.0),
        "g_pre_mix": gain(ks[1], (L, D_MODEL)),
        "w_in": nrm(ks[2], (L, D_MODEL, IN_COLS), D_MODEL ** -0.5),
        "conv_w": nrm(ks[3], (L, CONV_WIDTH, CONV_CH), CONV_WIDTH ** -0.5),
        "conv_b": nrm(ks[4], (L, CONV_CH), 0.02),
        "conv_ln_g": gain(ks[5], (L, CONV_CH)),
        "conv_ln_b": nrm(ks[6], (L, CONV_CH), 0.02),
        "g_q_a": gain(ks[7], (L, MLA_Q_RANK)),
        "w_uq": nrm(ks[8], (L, MLA_Q_RANK, MLA_HEADS * (MLA_NOPE + MLA_ROPE)), MLA_Q_RANK ** -0.5),
        "g_kv_a": gain(ks[9], (L, MLA_KV_RANK)),
        "w_ukv": nrm(ks[10], (L, MLA_KV_RANK, MLA_HEADS * (MLA_NOPE + MLA_V)), MLA_KV_RANK ** -0.5),
        "na_rpb": nrm(ks[11], (L, NA_HEADS, 2 * NA_ROWS_MAX - 1, 2 * NA_COLS - 1), 0.1),
        "g_out_conv": gain(ks[12], (L, MIX_CONV)),
        "g_out_mla": gain(ks[13], (L, MIX_MLA)),
        "g_out_na": gain(ks[14], (L, MIX_NA)),
        "w_o": nrm(ks[15], (L, MIX_WIDTH, D_MODEL), MIX_WIDTH ** -0.5),
        "g_post_mix": gain(ks[16], (L, D_MODEL)),
        "g_pre_ffn": gain(ks[17], (L, D_MODEL)),
        "w_up": nrm(ks[18], (L, D_MODEL, D_FF), D_MODEL ** -0.5),
        "w_down": nrm(ks[19], (L, D_FF, D_MODEL), D_FF ** -0.5),
        "g_post_ffn": gain(ks[20], (L, D_MODEL)),
    }


def reference(x, g_pre_mix, w_in, conv_w, conv_b, conv_ln_g, conv_ln_b, g_q_a, w_uq,
              g_kv_a, w_ukv, na_rpb, g_out_conv, g_out_mla, g_out_na, w_o, g_post_mix,
              g_pre_ffn, w_up, w_down, g_post_ffn):
    S = x.shape[1]
    cos, sin = rope_tables(S)
    for l in range(DEPTH):
        h = rms_norm(x, g_pre_mix[l])
        proj = h @ w_in[l]
        u_conv, c_q, c_kv, k_rope, q_na, k_na, v_na = jnp.split(proj, IN_SPLIT_POINTS, axis=-1)
        y_conv = conformer_conv(u_conv, conv_w[l], conv_b[l], conv_ln_g[l], conv_ln_b[l])
        y_mla = mla_attention(c_q, c_kv, k_rope, g_q_a[l], w_uq[l], g_kv_a[l], w_ukv[l], cos, sin)
        y_na = neighbourhood_attention(q_na, k_na, v_na, na_rpb[l])
        mixed = jnp.concatenate([rms_norm(y_conv, g_out_conv[l]),
                                 rms_norm(y_mla, g_out_mla[l]),
                                 rms_norm(y_na, g_out_na[l])], axis=-1)
        x = x + rms_norm(mixed @ w_o[l], g_post_mix[l])
        h = rms_norm(x, g_pre_ffn[l])
        f = jnp.square(jax.nn.relu(h @ w_up[l])) @ w_down[l]
        x = x + rms_norm(f, g_post_ffn[l])
    return x
```

```python
import functools

import jax
import jax.numpy as jnp
from jax import lax
from jax.experimental import pallas as pl
from jax.experimental.pallas import tpu as pltpu

D_MODEL = 2048
SEQ = 2048
GRID_W = 64
GRID_ROWS = SEQ // GRID_W
EPS = 1e-6
CONV_CH = 512
CONV_WIDTH = 31
CONV_PAD = (CONV_WIDTH - 1) // 2
MLA_HEADS = 8
MLA_Q_RANK = 512
MLA_KV_RANK = 256
MLA_NOPE = 128
MLA_ROPE = 64
MLA_V = 128
ROPE_THETA = 10000.0
NA_HEADS = 8
NA_HEAD_DIM = 64
NA_ROWS = 8
NA_COLS = 16
RPB_ROWS = 2 * NA_ROWS - 1
RPB_COLS = 2 * NA_COLS - 1
MIX_NA = NA_HEADS * NA_HEAD_DIM
MIX_MLA = MLA_HEADS * MLA_V
D_FF = 4 * D_MODEL

LANES = 128
SUBLANES = 8
VMEM_LIMIT_BYTES = 56 * 1024 * 1024

COL_CQ = 2 * CONV_CH
COL_CKV = COL_CQ + MLA_Q_RANK
COL_KR = COL_CKV + MLA_KV_RANK
COL_QNA = COL_KR + 2 * MLA_ROPE
COL_KNA = COL_QNA + MIX_NA
COL_VNA = COL_KNA + MIX_NA
IN_COLS_EXT = COL_VNA + MIX_NA

TM_IN = 512
TN_IN = IN_COLS_EXT // 3
CONV_ROWS = 128
CONV_HALO = 16
TM_UP = 512
TQ_MLA = 512
NA_QROWS = 4
NA_KROWS = 12
NA_QB = NA_QROWS * GRID_W
NA_KB = NA_KROWS * GRID_W
TM_OUT = 512
TM_FFN = 512
TK_FFN = 1024
MASK_VALUE = -1e30

F32 = jnp.float32
BF16 = jnp.bfloat16


def _params(*semantics):
    return pltpu.CompilerParams(dimension_semantics=semantics, vmem_limit_bytes=VMEM_LIMIT_BYTES)


def _rms(x, g):
    return x * lax.rsqrt(jnp.mean(x * x, axis=-1, keepdims=True) + EPS) * g


def _sigmoid(x):
    return 1.0 / (1.0 + jnp.exp(-x))


def _in_proj_kernel(x_ref, g_ref, w_ref, o_ref, h_ref):
    @pl.when(pl.program_id(1) == 0)
    def _():
        h_ref[...] = _rms(x_ref[...], g_ref[...]).astype(BF16)

    o_ref[...] = jnp.dot(h_ref[...], w_ref[...], preferred_element_type=F32).astype(o_ref.dtype)


def _in_proj(x2d, g, w_ext):
    t = x2d.shape[0]
    return pl.pallas_call(
        _in_proj_kernel,
        name="in_proj",
        grid=(t // TM_IN, IN_COLS_EXT // TN_IN),
        in_specs=[
            pl.BlockSpec((TM_IN, D_MODEL), lambda i, j: (i, 0)),
            pl.BlockSpec((1, D_MODEL), lambda i, j: (0, 0)),
            pl.BlockSpec((D_MODEL, TN_IN), lambda i, j: (0, j)),
        ],
        out_specs=pl.BlockSpec((TM_IN, TN_IN), lambda i, j: (i, j)),
        out_shape=jax.ShapeDtypeStruct((t, IN_COLS_EXT), BF16),
        scratch_shapes=[pltpu.VMEM((TM_IN, D_MODEL), BF16)],
        compiler_params=_params("parallel", "arbitrary"),
    )(x2d, g, w_ext)


def _conv_kernel(a_ref, gate_ref, w_ref, b_ref, lg_ref, lb_ref, o_ref, hpad_ref):
    n_tiles = SEQ // CONV_ROWS
    zeros = jnp.zeros((CONV_HALO, CONV_CH), F32)
    hpad_ref[0:CONV_HALO, :] = zeros
    hpad_ref[CONV_HALO + SEQ:2 * CONV_HALO + SEQ, :] = zeros

    def glu(t, carry):
        base = pl.multiple_of(t * CONV_ROWS, CONV_ROWS)
        a = a_ref[pl.ds(base, CONV_ROWS), :].astype(F32)
        gt = gate_ref[pl.ds(base, CONV_ROWS), :].astype(F32)
        hpad_ref[pl.ds(base + CONV_HALO, CONV_ROWS), :] = a * _sigmoid(gt)
        return carry

    lax.fori_loop(0, n_tiles, glu, 0)

    def tile(t, carry):
        base = pl.multiple_of(t * CONV_ROWS, CONV_ROWS)
        win_rows = CONV_ROWS + 2 * CONV_HALO
        accs = []
        for c in range(CONV_CH // LANES):
            cols = slice(c * LANES, (c + 1) * LANES)
            win = hpad_ref[pl.ds(base, win_rows), cols]
            acc = jnp.zeros((CONV_ROWS, LANES), F32)
            for r in range(SUBLANES):
                win_r = win if r == 0 else pltpu.roll(win, win_rows - r, axis=0)
                for q in range(2 * CONV_HALO // SUBLANES):
                    k = SUBLANES * q + r - (CONV_HALO - CONV_PAD)
                    if 0 <= k < CONV_WIDTH:
                        acc = acc + win_r[SUBLANES * q:SUBLANES * q + CONV_ROWS, :] * w_ref[k:k + 1, cols]
            accs.append(acc)
        acc = jnp.concatenate(accs, axis=1) + b_ref[...]
        mu = jnp.mean(acc, axis=-1, keepdims=True)
        xc = acc - mu
        var = jnp.mean(xc * xc, axis=-1, keepdims=True)
        y = xc * lax.rsqrt(var + EPS) * lg_ref[...] + lb_ref[...]
        o_ref[pl.ds(base, CONV_ROWS), :] = (y * _sigmoid(y)).astype(o_ref.dtype)
        return carry

    lax.fori_loop(0, n_tiles, tile, 0)


def _conv(proj, conv_w, conv_b, ln_g, ln_b):
    t = proj.shape[0]
    vec = pl.BlockSpec((1, CONV_CH), lambda b: (0, 0))
    return pl.pallas_call(
        _conv_kernel,
        name="conformer_conv",
        grid=(t // SEQ,),
        in_specs=[
            pl.BlockSpec((SEQ, CONV_CH), lambda b: (b, 0)),
            pl.BlockSpec((SEQ, CONV_CH), lambda b: (b, 1)),
            pl.BlockSpec((CONV_WIDTH, CONV_CH), lambda b: (0, 0)),
            vec, vec, vec,
        ],
        out_specs=pl.BlockSpec((SEQ, CONV_CH), lambda b: (b, 0)),
        out_shape=jax.ShapeDtypeStruct((t, CONV_CH), BF16),
        scratch_shapes=[pltpu.VMEM((SEQ + 2 * CONV_HALO, CONV_CH), F32)],
        compiler_params=_params("parallel"),
    )(proj, proj, conv_w, conv_b, ln_g, ln_b)


def _rope_pair(block, cs):
    t = block * cs
    return t + pltpu.roll(t, MLA_ROPE, axis=1)


def _mla_up_kernel(cq_ref, ckv_ref, kr_ref, cs_ref, gq_ref, gkv_ref, wq_ref, wkv_ref,
                   q_ref, k_ref, v_ref, hq_ref, hkv_ref, krot_ref):
    scale = (MLA_NOPE + MLA_ROPE) ** -0.5

    @pl.when(pl.program_id(1) == 0)
    def _():
        hq_ref[...] = _rms(cq_ref[...].astype(F32), gq_ref[...]).astype(BF16)
        hkv_ref[...] = _rms(ckv_ref[...].astype(F32), gkv_ref[...]).astype(BF16)
        kr = _rope_pair(kr_ref[...].astype(F32), cs_ref[...])
        lane = lax.broadcasted_iota(jnp.int32, kr.shape, 1)
        krot_ref[...] = jnp.where(lane < MLA_ROPE, kr, 0.0).astype(BF16)

    rq = jnp.dot(hq_ref[...], wq_ref[0], preferred_element_type=F32)
    q_ref[0, 0, :, 0:MLA_NOPE] = (rq[:, 0:MLA_NOPE] * scale).astype(BF16)
    q_ref[0, 0, :, MLA_NOPE:2 * MLA_NOPE] = (_rope_pair(rq[:, MLA_NOPE:2 * MLA_NOPE], cs_ref[...]) * scale).astype(BF16)
    rkv = jnp.dot(hkv_ref[...], wkv_ref[0], preferred_element_type=F32)
    k_ref[0, 0, :, 0:MLA_NOPE] = rkv[:, 0:MLA_NOPE].astype(BF16)
    k_ref[0, 0, :, MLA_NOPE:2 * MLA_NOPE] = krot_ref[...]
    v_ref[0, 0] = rkv[:, MLA_NOPE:MLA_NOPE + MLA_V].astype(BF16)


def _mla_up(proj, cs_table, g_q, g_kv, wq_heads, wkv_heads):
    t = proj.shape[0]
    batch = t // SEQ
    n_s = SEQ // TM_UP
    qk_spec = pl.BlockSpec((1, 1, TM_UP, 2 * MLA_NOPE), lambda i, h: (i // n_s, h, i % n_s, 0))
    return pl.pallas_call(
        _mla_up_kernel,
        name="mla_up",
        grid=(t // TM_UP, MLA_HEADS),
        in_specs=[
            pl.BlockSpec((TM_UP, MLA_Q_RANK), lambda i, h: (i, COL_CQ // MLA_Q_RANK)),
            pl.BlockSpec((TM_UP, MLA_KV_RANK), lambda i, h: (i, COL_CKV // MLA_KV_RANK)),
            pl.BlockSpec((TM_UP, 2 * MLA_ROPE), lambda i, h: (i, COL_KR // (2 * MLA_ROPE))),
            pl.BlockSpec((TM_UP, 2 * MLA_ROPE), lambda i, h: (i % n_s, 0)),
            pl.BlockSpec((1, MLA_Q_RANK), lambda i, h: (0, 0)),
            pl.BlockSpec((1, MLA_KV_RANK), lambda i, h: (0, 0)),
            pl.BlockSpec((1, MLA_Q_RANK, 2 * MLA_NOPE), lambda i, h: (h, 0, 0)),
            pl.BlockSpec((1, MLA_KV_RANK, MLA_NOPE + MLA_V), lambda i, h: (h, 0, 0)),
        ],
        out_specs=[
            qk_spec,
            qk_spec,
            pl.BlockSpec((1, 1, TM_UP, MLA_V), lambda i, h: (i // n_s, h, i % n_s, 0)),
        ],
        out_shape=[
            jax.ShapeDtypeStruct((batch, MLA_HEADS, SEQ, 2 * MLA_NOPE), BF16),
            jax.ShapeDtypeStruct((batch, MLA_HEADS, SEQ, 2 * MLA_NOPE), BF16),
            jax.ShapeDtypeStruct((batch, MLA_HEADS, SEQ, MLA_V), BF16),
        ],
        scratch_shapes=[
            pltpu.VMEM((TM_UP, MLA_Q_RANK), BF16),
            pltpu.VMEM((TM_UP, MLA_KV_RANK), BF16),
            pltpu.VMEM((TM_UP, 2 * MLA_ROPE), BF16),
        ],
        compiler_params=_params("parallel", "arbitrary"),
    )(proj, proj, proj, cs_table, g_q, g_kv, wq_heads, wkv_heads)


def _mla_attn_kernel(q_ref, k_ref, v_ref, o_ref):
    s = lax.dot_general(q_ref[0, 0], k_ref[0, 0], (((1,), (1,)), ((), ())),
                        preferred_element_type=F32)
    m = jnp.max(s, axis=-1, keepdims=True)
    p = jnp.exp(s - m)
    l = jnp.sum(p, axis=-1, keepdims=True)
    o = jnp.dot(p.astype(BF16), v_ref[0, 0], preferred_element_type=F32)
    o_ref[...] = (o / l).astype(o_ref.dtype)


def _mla_attn(q, k, v):
    batch = q.shape[0]
    n_q = SEQ // TQ_MLA
    return pl.pallas_call(
        _mla_attn_kernel,
        name="mla_attn",
        grid=(batch, MLA_HEADS, n_q),
        in_specs=[
            pl.BlockSpec((1, 1, TQ_MLA, 2 * MLA_NOPE), lambda b, h, i: (b, h, i, 0)),
            pl.BlockSpec((1, 1, SEQ, 2 * MLA_NOPE), lambda b, h, i: (b, h, 0, 0)),
            pl.BlockSpec((1, 1, SEQ, MLA_V), lambda b, h, i: (b, h, 0, 0)),
        ],
        out_specs=pl.BlockSpec((TQ_MLA, MLA_V), lambda b, h, i: (b * n_q + i, h)),
        out_shape=jax.ShapeDtypeStruct((batch * SEQ, MIX_MLA), BF16),
        compiler_params=_params("parallel", "parallel", "parallel"),
    )(q, k, v)


def _na_pattern(pat, i, j):
    if pat == 0:
        return j <= NA_ROWS - 1, j - i + (NA_ROWS - 1)
    if pat == 1:
        return i <= j <= i + NA_ROWS - 1, j - i + (NA_ROWS - 1) - NA_QROWS
    return NA_KROWS - NA_ROWS <= j, j - i - 1


def _na_build_tables(pair, rpb_ref, toep_ref, tab_ref):
    shape = (GRID_W, LANES)
    w_q = lax.broadcasted_iota(jnp.int32, shape, 0)
    lane = lax.broadcasted_iota(jnp.int32, shape, 1)
    w_k = lane & (GRID_W - 1)
    dc = w_k - w_q + (NA_COLS - 1)
    col_start = jnp.clip(w_q - NA_COLS // 2, 0, GRID_W - NA_COLS)
    col_valid = (w_k >= col_start) & (w_k < col_start + NA_COLS)
    left = lane < GRID_W
    neg = jnp.full(shape, MASK_VALUE, F32)
    for hh in range(2):
        head = pair * 2 + hh
        for a in range(RPB_ROWS):
            acc = jnp.zeros(shape, F32)
            for bb in range(RPB_COLS):
                acc = jnp.where(dc == bb, rpb_ref[head * (RPB_ROWS * RPB_COLS) + a * RPB_COLS + bb], acc)
            toep_ref[hh, a] = acc
        for pat in range(3):
            for i in range(NA_QROWS):
                for jp in range(NA_KROWS // 2):
                    v_l, a_l = _na_pattern(pat, i, 2 * jp)
                    v_r, a_r = _na_pattern(pat, i, 2 * jp + 1)
                    if not (v_l or v_r):
                        tile = neg
                    else:
                        t_l = toep_ref[hh, a_l] if v_l else neg
                        t_r = toep_ref[hh, a_r] if v_r else neg
                        tile = jnp.where(col_valid, jnp.where(left, t_l, t_r), neg)
                    tab_ref[hh, pat, i * GRID_W:(i + 1) * GRID_W, jp * LANES:(jp + 1) * LANES] = tile


def _na_kernel(rpb_ref, q_ref, k_ref, v_ref, o_ref, toep_ref, tab_ref):
    pair = pl.program_id(0)
    qb = pl.program_id(2)
    n_qb = GRID_ROWS // NA_QROWS

    @pl.when((pl.program_id(1) == 0) & (qb == 0))
    def _():
        _na_build_tables(pair, rpb_ref, toep_ref, tab_ref)

    start = pl.multiple_of(jnp.clip(qb - 1, 0, n_qb - NA_KROWS // NA_QROWS) * NA_QB, NA_QB)
    k_win = k_ref[pl.ds(start, NA_KB), :]
    v_win = v_ref[pl.ds(start, NA_KB), :]
    pat = jnp.where(qb == 0, 0, jnp.where(qb == n_qb - 1, 2, 1))
    q = q_ref[...]
    lane = lax.broadcasted_iota(jnp.int32, q.shape, 1)
    left = lane < NA_HEAD_DIM
    outs = []
    for hh in range(2):
        q_h = jnp.where(left if hh == 0 else jnp.logical_not(left), q, jnp.zeros_like(q))
        s = lax.dot_general(q_h, k_win, (((1,), (1,)), ((), ())), preferred_element_type=F32)
        s = s * (NA_HEAD_DIM ** -0.5) + tab_ref[hh, pat]
        m = jnp.max(s, axis=-1, keepdims=True)
        p = jnp.exp(s - m)
        l = jnp.sum(p, axis=-1, keepdims=True)
        outs.append(jnp.dot(p.astype(BF16), v_win, preferred_element_type=F32) / l)
    o_ref[...] = jnp.where(left, outs[0], outs[1]).astype(o_ref.dtype)


def _na_attn(proj, rpb_flat):
    t = proj.shape[0]
    batch = t // SEQ
    n_qb = GRID_ROWS // NA_QROWS
    n_pairs = NA_HEADS // 2
    return pl.pallas_call(
        _na_kernel,
        name="na_attn",
        grid=(n_pairs, batch, n_qb),
        in_specs=[
            pl.BlockSpec(memory_space=pltpu.SMEM),
            pl.BlockSpec((NA_QB, LANES), lambda p, b, i: (b * n_qb + i, COL_QNA // LANES + p)),
            pl.BlockSpec((SEQ, LANES), lambda p, b, i: (b, COL_KNA // LANES + p)),
            pl.BlockSpec((SEQ, LANES), lambda p, b, i: (b, COL_VNA // LANES + p)),
        ],
        out_specs=pl.BlockSpec((NA_QB, LANES), lambda p, b, i: (b * n_qb + i, p)),
        out_shape=jax.ShapeDtypeStruct((t, MIX_NA), BF16),
        scratch_shapes=[
            pltpu.VMEM((2, RPB_ROWS, GRID_W, LANES), F32),
            pltpu.VMEM((2, 3, NA_QB, NA_KB), F32),
        ],
        compiler_params=_params("arbitrary", "arbitrary", "arbitrary"),
    )(rpb_flat, proj, proj, proj)


def _out_proj_kernel(yc_ref, ym_ref, yn_ref, gc_ref, gm_ref, gn_ref, w_ref, x_ref, gp_ref,
                     o_ref, mix_ref):
    mix_ref[:, 0:CONV_CH] = _rms(yc_ref[...].astype(F32), gc_ref[...]).astype(BF16)
    mix_ref[:, CONV_CH:CONV_CH + MIX_MLA] = _rms(ym_ref[...].astype(F32), gm_ref[...]).astype(BF16)
    mix_ref[:, CONV_CH + MIX_MLA:] = _rms(yn_ref[...].astype(F32), gn_ref[...]).astype(BF16)
    f = jnp.dot(mix_ref[...], w_ref[...], preferred_element_type=F32)
    o_ref[...] = x_ref[...] + _rms(f, gp_ref[...])


def _out_proj(y_conv, y_mla, y_na, g_c, g_m, g_n, w_o, x2d, g_post):
    t = x2d.shape[0]
    mix_width = CONV_CH + MIX_MLA + MIX_NA

    def rows(width):
        return pl.BlockSpec((TM_OUT, width), lambda i: (i, 0))

    def vec(width):
        return pl.BlockSpec((1, width), lambda i: (0, 0))

    return pl.pallas_call(
        _out_proj_kernel,
        name="out_proj",
        grid=(t // TM_OUT,),
        in_specs=[
            rows(CONV_CH), rows(MIX_MLA), rows(MIX_NA),
            vec(CONV_CH), vec(MIX_MLA), vec(MIX_NA),
            pl.BlockSpec((mix_width, D_MODEL), lambda i: (0, 0)),
            rows(D_MODEL), vec(D_MODEL),
        ],
        out_specs=rows(D_MODEL),
        out_shape=jax.ShapeDtypeStruct((t, D_MODEL), F32),
        scratch_shapes=[pltpu.VMEM((TM_OUT, mix_width), BF16)],
        compiler_params=_params("parallel"),
    )(y_conv, y_mla, y_na, g_c, g_m, g_n, w_o, x2d, g_post)


def _ffn_kernel(x_ref, gpre_ref, wup_ref, wdn_ref, gpost_ref, o_ref, h_ref, acc_ref):
    k = pl.program_id(1)

    @pl.when(k == 0)
    def _():
        h_ref[...] = _rms(x_ref[...], gpre_ref[...]).astype(BF16)

    a = jnp.maximum(jnp.dot(h_ref[...], wup_ref[...], preferred_element_type=F32), 0.0)
    part = jnp.dot((a * a).astype(BF16), wdn_ref[...], preferred_element_type=F32)

    @pl.when(k == 0)
    def _():
        acc_ref[...] = part

    @pl.when(k > 0)
    def _():
        acc_ref[...] += part

    @pl.when(k == pl.num_programs(1) - 1)
    def _():
        o_ref[...] = x_ref[...] + _rms(acc_ref[...], gpost_ref[...])


def _ffn(x2d, g_pre, w_up, w_down, g_post):
    t = x2d.shape[0]
    vec = pl.BlockSpec((1, D_MODEL), lambda i, k: (0, 0))
    return pl.pallas_call(
        _ffn_kernel,
        name="ffn",
        grid=(t // TM_FFN, D_FF // TK_FFN),
        in_specs=[
            pl.BlockSpec((TM_FFN, D_MODEL), lambda i, k: (i, 0)),
            vec,
            pl.BlockSpec((D_MODEL, TK_FFN), lambda i, k: (0, k)),
            pl.BlockSpec((TK_FFN, D_MODEL), lambda i, k: (k, 0)),
            vec,
        ],
        out_specs=pl.BlockSpec((TM_FFN, D_MODEL), lambda i, k: (i, 0)),
        out_shape=jax.ShapeDtypeStruct((t, D_MODEL), F32),
        scratch_shapes=[pltpu.VMEM((TM_FFN, D_MODEL), BF16), pltpu.VMEM((TM_FFN, D_MODEL), F32)],
        compiler_params=_params("parallel", "arbitrary"),
    )(x2d, g_pre, w_up, w_down, g_post)


def _rotate_half_cols(w):
    half = MLA_ROPE // 2
    return jnp.concatenate([-w[..., half:], w[..., :half]], axis=-1)


def _prep_w_in(w_in):
    bounds = (0, COL_CQ, COL_CKV, COL_KR, COL_KR + MLA_ROPE)
    u_conv, c_q, c_kv, k_rope = (w_in[:, bounds[n]:bounds[n + 1]] for n in range(4))
    rest = w_in[:, bounds[4]:]
    return jnp.concatenate([u_conv, c_q, c_kv, k_rope, _rotate_half_cols(k_rope), rest], axis=1).astype(BF16)


def _prep_w_uq(w_uq):
    w = w_uq.reshape(MLA_Q_RANK, MLA_HEADS, MLA_NOPE + MLA_ROPE)
    pe = w[..., MLA_NOPE:]
    w = jnp.concatenate([w[..., :MLA_NOPE], pe, _rotate_half_cols(pe)], axis=-1)
    return w.transpose(1, 0, 2).astype(BF16)


def _prep_w_ukv(w_ukv):
    w = w_ukv.reshape(MLA_KV_RANK, MLA_HEADS, MLA_NOPE + MLA_V)
    return w.transpose(1, 0, 2).astype(BF16)


def _rope_table():
    pos = jnp.arange(SEQ, dtype=F32)
    inv_freq = 1.0 / (ROPE_THETA ** (jnp.arange(0, MLA_ROPE, 2, dtype=F32) / MLA_ROPE))
    ang = pos[:, None] * inv_freq[None, :]
    cos, sin = jnp.cos(ang), jnp.sin(ang)
    return jnp.concatenate([cos, cos, sin, sin], axis=-1)


def kernel(x, g_pre_mix, w_in, conv_w, conv_b, conv_ln_g, conv_ln_b, g_q_a, w_uq, g_kv_a, w_ukv,
           na_rpb, g_out_conv, g_out_mla, g_out_na, w_o, g_post_mix, g_pre_ffn, w_up, w_down,
           g_post_ffn):
    batch, seq, d_model = x.shape
    assert (seq, d_model) == (SEQ, D_MODEL)
    depth = w_in.shape[0]
    cs_table = _rope_table()
    x2d = x.reshape(batch * seq, d_model)
    for l in range(depth):
        proj = _in_proj(x2d, g_pre_mix[l][None], _prep_w_in(w_in[l]))
        y_conv = _conv(proj, conv_w[l], conv_b[l][None], conv_ln_g[l][None], conv_ln_b[l][None])
        q, k, v = _mla_up(proj, cs_table, g_q_a[l][None], g_kv_a[l][None],
                          _prep_w_uq(w_uq[l]), _prep_w_ukv(w_ukv[l]))
        y_mla = _mla_attn(q, k, v)
        y_na = _na_attn(proj, na_rpb[l].reshape(-1))
        x2d = _out_proj(y_conv, y_mla, y_na, g_out_conv[l][None], g_out_mla[l][None],
                        g_out_na[l][None], w_o[l].astype(BF16), x2d, g_post_mix[l][None])
        x2d = _ffn(x2d, g_pre_ffn[l][None], w_up[l].astype(BF16), w_down[l].astype(BF16),
                   g_post_ffn[l][None])
    return x2d.reshape(batch, seq, d_model)
```

```python
import functools

import jax
import jax.numpy as jnp
from jax import lax
from jax.experimental import pallas as pl
from jax.experimental.pallas import tpu as pltpu

D_MODEL = 2048
SEQ = 2048
GRID_W = 64
GRID_ROWS = SEQ // GRID_W
EPS = 1e-6
CONV_CH = 512
CONV_WIDTH = 31
CONV_PAD = (CONV_WIDTH - 1) // 2
MLA_HEADS = 8
MLA_Q_RANK = 512
MLA_KV_RANK = 256
MLA_NOPE = 128
MLA_ROPE = 64
MLA_V = 128
ROPE_THETA = 10000.0
NA_HEADS = 8
NA_HEAD_DIM = 64
NA_ROWS = 8
NA_COLS = 16
RPB_ROWS = 2 * NA_ROWS - 1
RPB_COLS = 2 * NA_COLS - 1
MIX_NA = NA_HEADS * NA_HEAD_DIM
MIX_MLA = MLA_HEADS * MLA_V
D_FF = 4 * D_MODEL

LANES = 128
SUBLANES = 8
VMEM_LIMIT_BYTES = 56 * 1024 * 1024

COL_CQ = 2 * CONV_CH
COL_QNA = COL_CQ + MLA_Q_RANK
COL_KNA = COL_QNA + MIX_NA
COL_VNA = COL_KNA + MIX_NA
COL_CKV = COL_VNA + MIX_NA
COL_KR = COL_CKV + MLA_KV_RANK
IN_COLS_EXT = COL_KR + 2 * MLA_ROPE
assert COL_CQ % MLA_Q_RANK == 0 and COL_CKV % MLA_KV_RANK == 0 and COL_KR % (2 * MLA_ROPE) == 0
assert COL_QNA % MIX_NA == 0 and COL_KNA % MIX_NA == 0 and COL_VNA % MIX_NA == 0

TM_IN = 512
TN_IN = IN_COLS_EXT // 3
CONV_ROWS = 128
CONV_HALO = 16
TM_UP = 512
TQ_MLA = 512
NA_QROWS = 4
NA_KROWS = 12
NA_QB = NA_QROWS * GRID_W
NA_KB = NA_KROWS * GRID_W
TM_OUT = 512
TM_FFN = 512
TK_FFN = 1024
MASK_VALUE = -1e30

F32 = jnp.float32
BF16 = jnp.bfloat16


def _params(*semantics):
    return pltpu.CompilerParams(dimension_semantics=semantics, vmem_limit_bytes=VMEM_LIMIT_BYTES)


def _layer_vec(l, width):
    return pl.BlockSpec((None, 1, width), lambda *_: (l, 0, 0))


def _rms(x, g):
    return x * lax.rsqrt(jnp.mean(x * x, axis=-1, keepdims=True) + EPS) * g


def _sigmoid(x):
    return 1.0 / (1.0 + jnp.exp(-x))


def _in_proj_kernel(x_ref, g_ref, w_ref, o_ref, h_ref):
    @pl.when(pl.program_id(1) == 0)
    def _():
        h_ref[...] = _rms(x_ref[...], g_ref[...]).astype(BF16)

    o_ref[...] = jnp.dot(h_ref[...], w_ref[...], preferred_element_type=F32).astype(o_ref.dtype)


def _in_proj(l, x2d, g, w_ext):
    t = x2d.shape[0]
    return pl.pallas_call(
        _in_proj_kernel,
        name="in_proj",
        grid=(t // TM_IN, IN_COLS_EXT // TN_IN),
        in_specs=[
            pl.BlockSpec((TM_IN, D_MODEL), lambda i, j: (i, 0)),
            _layer_vec(l, D_MODEL),
            pl.BlockSpec((None, D_MODEL, TN_IN), lambda i, j: (l, 0, j)),
        ],
        out_specs=pl.BlockSpec((TM_IN, TN_IN), lambda i, j: (i, j)),
        out_shape=jax.ShapeDtypeStruct((t, IN_COLS_EXT), BF16),
        scratch_shapes=[pltpu.VMEM((TM_IN, D_MODEL), BF16)],
        compiler_params=_params("parallel", "arbitrary"),
    )(x2d, g, w_ext)


def _conv_kernel(a_ref, gate_ref, w_ref, b_ref, lg_ref, lb_ref, o_ref, hpad_ref):
    n_tiles = SEQ // CONV_ROWS
    zeros = jnp.zeros((CONV_HALO, CONV_CH), F32)
    hpad_ref[0:CONV_HALO, :] = zeros
    hpad_ref[CONV_HALO + SEQ:2 * CONV_HALO + SEQ, :] = zeros

    def glu(t, carry):
        base = pl.multiple_of(t * CONV_ROWS, CONV_ROWS)
        a = a_ref[pl.ds(base, CONV_ROWS), :].astype(F32)
        gt = gate_ref[pl.ds(base, CONV_ROWS), :].astype(F32)
        hpad_ref[pl.ds(base + CONV_HALO, CONV_ROWS), :] = a * _sigmoid(gt)
        return carry

    lax.fori_loop(0, n_tiles, glu, 0)

    def tile(t, carry):
        base = pl.multiple_of(t * CONV_ROWS, CONV_ROWS)
        win_rows = CONV_ROWS + 2 * CONV_HALO
        accs = []
        for c in range(CONV_CH // LANES):
            cols = slice(c * LANES, (c + 1) * LANES)
            win = hpad_ref[pl.ds(base, win_rows), cols]
            acc = jnp.zeros((CONV_ROWS, LANES), F32)
            for r in range(SUBLANES):
                win_r = win if r == 0 else pltpu.roll(win, win_rows - r, axis=0)
                for q in range(2 * CONV_HALO // SUBLANES):
                    k = SUBLANES * q + r - (CONV_HALO - CONV_PAD)
                    if 0 <= k < CONV_WIDTH:
                        acc = acc + win_r[SUBLANES * q:SUBLANES * q + CONV_ROWS, :] * w_ref[k:k + 1, cols]
            accs.append(acc)
        acc = jnp.concatenate(accs, axis=1) + b_ref[...]
        mu = jnp.mean(acc, axis=-1, keepdims=True)
        xc = acc - mu
        var = jnp.mean(xc * xc, axis=-1, keepdims=True)
        y = xc * lax.rsqrt(var + EPS) * lg_ref[...] + lb_ref[...]
        o_ref[pl.ds(base, CONV_ROWS), :] = (y * _sigmoid(y)).astype(o_ref.dtype)
        return carry

    lax.fori_loop(0, n_tiles, tile, 0)


def _conv(l, proj, conv_w, conv_b, ln_g, ln_b):
    t = proj.shape[0]
    vec = _layer_vec(l, CONV_CH)
    return pl.pallas_call(
        _conv_kernel,
        name="conformer_conv",
        grid=(t // SEQ,),
        in_specs=[
            pl.BlockSpec((SEQ, CONV_CH), lambda b: (b, 0)),
            pl.BlockSpec((SEQ, CONV_CH), lambda b: (b, 1)),
            pl.BlockSpec((None, CONV_WIDTH, CONV_CH), lambda b: (l, 0, 0)),
            vec, vec, vec,
        ],
        out_specs=pl.BlockSpec((SEQ, CONV_CH), lambda b: (b, 0)),
        out_shape=jax.ShapeDtypeStruct((t, CONV_CH), BF16),
        scratch_shapes=[pltpu.VMEM((SEQ + 2 * CONV_HALO, CONV_CH), F32)],
        compiler_params=_params("parallel"),
    )(proj, proj, conv_w, conv_b, ln_g, ln_b)


def _rope_pair(block, cs):
    t = block * cs
    return t + pltpu.roll(t, MLA_ROPE, axis=1)


def _mla_up_kernel(cq_ref, ckv_ref, kr_ref, cs_ref, gq_ref, gkv_ref, wq_ref, wkv_ref,
                   q_ref, k_ref, v_ref):
    scale = (MLA_NOPE + MLA_ROPE) ** -0.5
    cs = cs_ref[...]
    hq = _rms(cq_ref[...].astype(F32), gq_ref[...]).astype(BF16)
    hkv = _rms(ckv_ref[...].astype(F32), gkv_ref[...]).astype(BF16)
    kr = _rope_pair(kr_ref[...].astype(F32), cs)
    lane = lax.broadcasted_iota(jnp.int32, kr.shape, 1)
    k_rot = jnp.where(lane < MLA_ROPE, kr, 0.0).astype(BF16)
    for h in range(MLA_HEADS):
        rq = jnp.dot(hq, wq_ref[h], preferred_element_type=F32)
        q_ref[0, h, :, 0:MLA_NOPE] = (rq[:, 0:MLA_NOPE] * scale).astype(BF16)
        q_ref[0, h, :, MLA_NOPE:2 * MLA_NOPE] = (_rope_pair(rq[:, MLA_NOPE:2 * MLA_NOPE], cs) * scale).astype(BF16)
        rkv = jnp.dot(hkv, wkv_ref[h], preferred_element_type=F32)
        k_ref[0, h, :, 0:MLA_NOPE] = rkv[:, 0:MLA_NOPE].astype(BF16)
        k_ref[0, h, :, MLA_NOPE:2 * MLA_NOPE] = k_rot
        v_ref[0, h] = rkv[:, MLA_NOPE:MLA_NOPE + MLA_V].astype(BF16)


def _mla_up(l, proj, cs_table, g_q, g_kv, wq_heads, wkv_heads):
    t = proj.shape[0]
    batch = t // SEQ
    n_s = SEQ // TM_UP
    qk_spec = pl.BlockSpec((1, MLA_HEADS, TM_UP, 2 * MLA_NOPE), lambda i: (i // n_s, 0, i % n_s, 0))
    return pl.pallas_call(
        _mla_up_kernel,
        name="mla_up",
        grid=(t // TM_UP,),
        in_specs=[
            pl.BlockSpec((TM_UP, MLA_Q_RANK), lambda i: (i, COL_CQ // MLA_Q_RANK)),
            pl.BlockSpec((TM_UP, MLA_KV_RANK), lambda i: (i, COL_CKV // MLA_KV_RANK)),
            pl.BlockSpec((TM_UP, 2 * MLA_ROPE), lambda i: (i, COL_KR // (2 * MLA_ROPE))),
            pl.BlockSpec((TM_UP, 2 * MLA_ROPE), lambda i: (i % n_s, 0)),
            _layer_vec(l, MLA_Q_RANK),
            _layer_vec(l, MLA_KV_RANK),
            pl.BlockSpec((None, MLA_HEADS, MLA_Q_RANK, 2 * MLA_NOPE), lambda i: (l, 0, 0, 0)),
            pl.BlockSpec((None, MLA_HEADS, MLA_KV_RANK, MLA_NOPE + MLA_V), lambda i: (l, 0, 0, 0)),
        ],
        out_specs=[
            qk_spec,
            qk_spec,
            pl.BlockSpec((1, MLA_HEADS, TM_UP, MLA_V), lambda i: (i // n_s, 0, i % n_s, 0)),
        ],
        out_shape=[
            jax.ShapeDtypeStruct((batch, MLA_HEADS, SEQ, 2 * MLA_NOPE), BF16),
            jax.ShapeDtypeStruct((batch, MLA_HEADS, SEQ, 2 * MLA_NOPE), BF16),
            jax.ShapeDtypeStruct((batch, MLA_HEADS, SEQ, MLA_V), BF16),
        ],
        compiler_params=_params("parallel"),
    )(proj, proj, proj, cs_table, g_q, g_kv, wq_heads, wkv_heads)


def _mla_attn_kernel(q_ref, k_ref, v_ref, o_ref):
    for c in range(SEQ // TQ_MLA):
        rows = slice(c * TQ_MLA, (c + 1) * TQ_MLA)
        s = lax.dot_general(q_ref[0, 0, rows, :], k_ref[0, 0], (((1,), (1,)), ((), ())),
                            preferred_element_type=F32)
        m = jnp.max(s, axis=-1, keepdims=True)
        p = jnp.exp(s - m)
        l = jnp.sum(p, axis=-1, keepdims=True)
        o = jnp.dot(p.astype(BF16), v_ref[0, 0], preferred_element_type=F32)
        o_ref[rows, :] = (o / l).astype(o_ref.dtype)


def _mla_attn(q, k, v):
    batch = q.shape[0]
    qk_spec = pl.BlockSpec((1, 1, SEQ, 2 * MLA_NOPE), lambda b, h: (b, h, 0, 0))
    return pl.pallas_call(
        _mla_attn_kernel,
        name="mla_attn",
        grid=(batch, MLA_HEADS),
        in_specs=[
            qk_spec,
            qk_spec,
            pl.BlockSpec((1, 1, SEQ, MLA_V), lambda b, h: (b, h, 0, 0)),
        ],
        out_specs=pl.BlockSpec((SEQ, MLA_V), lambda b, h: (b, h)),
        out_shape=jax.ShapeDtypeStruct((batch * SEQ, MIX_MLA), BF16),
        compiler_params=_params("parallel", "parallel"),
    )(q, k, v)


def _na_pattern(pat, i, j):
    if pat == 0:
        return j <= NA_ROWS - 1, j - i + (NA_ROWS - 1)
    if pat == 1:
        return i <= j <= i + NA_ROWS - 1, j - i + (NA_ROWS - 1) - NA_QROWS
    return NA_KROWS - NA_ROWS <= j, j - i - 1


def _na_build_tables(rpb_base, rpb_ref, toep_ref, tab_ref):
    shape = (GRID_W, LANES)
    w_q = lax.broadcasted_iota(jnp.int32, shape, 0)
    lane = lax.broadcasted_iota(jnp.int32, shape, 1)
    w_k = lane & (GRID_W - 1)
    dc = w_k - w_q + (NA_COLS - 1)
    col_start = jnp.clip(w_q - NA_COLS // 2, 0, GRID_W - NA_COLS)
    col_valid = (w_k >= col_start) & (w_k < col_start + NA_COLS)
    left = lane < GRID_W
    neg = jnp.full(shape, MASK_VALUE, F32)

    def one_head(head, carry):
        for a in range(RPB_ROWS):
            acc = jnp.zeros(shape, F32)
            for bb in range(RPB_COLS):
                idx = rpb_base + head * (RPB_ROWS * RPB_COLS) + (a * RPB_COLS + bb)
                acc = jnp.where(dc == bb, rpb_ref[idx], acc)
            toep_ref[a] = acc
        for pat in range(3):
            for i in range(NA_QROWS):
                for jp in range(NA_KROWS // 2):
                    v_l, a_l = _na_pattern(pat, i, 2 * jp)
                    v_r, a_r = _na_pattern(pat, i, 2 * jp + 1)
                    if not (v_l or v_r):
                        tile = neg
                    else:
                        t_l = toep_ref[a_l] if v_l else neg
                        t_r = toep_ref[a_r] if v_r else neg
                        tile = jnp.where(col_valid, jnp.where(left, t_l, t_r), neg)
                    row0 = pl.multiple_of(head * NA_QB + i * GRID_W, GRID_W)
                    tab_ref[pat, pl.ds(row0, GRID_W), jp * LANES:(jp + 1) * LANES] = tile
        return carry

    lax.fori_loop(0, NA_HEADS, one_head, 0)


def _na_kernel(rpb_base, rpb_ref, q_ref, k_ref, v_ref, o_ref, toep_ref, tab_ref):
    qb = pl.program_id(1)
    n_qb = GRID_ROWS // NA_QROWS

    @pl.when((pl.program_id(0) == 0) & (qb == 0))
    def _():
        _na_build_tables(rpb_base, rpb_ref, toep_ref, tab_ref)

    start = pl.multiple_of(jnp.clip(qb - 1, 0, n_qb - NA_KROWS // NA_QROWS) * NA_QB, NA_QB)
    pat = jnp.where(qb == 0, 0, jnp.where(qb == n_qb - 1, 2, 1))
    lane = lax.broadcasted_iota(jnp.int32, (NA_QB, LANES), 1)
    left = lane < NA_HEAD_DIM
    for pair in range(NA_HEADS // 2):
        cols = slice(pair * LANES, (pair + 1) * LANES)
        k_win = k_ref[pl.ds(start, NA_KB), cols]
        v_win = v_ref[pl.ds(start, NA_KB), cols]
        q = q_ref[:, cols] * (NA_HEAD_DIM ** -0.5)
        zero = jnp.zeros_like(q)
        q2 = jnp.concatenate([jnp.where(left, q, zero), jnp.where(left, zero, q)], axis=0)
        s = lax.dot_general(q2, k_win, (((1,), (1,)), ((), ())), preferred_element_type=F32)
        s = s + tab_ref[pat, pair * 2 * NA_QB:(pair + 1) * 2 * NA_QB, :]
        m = jnp.max(s, axis=-1, keepdims=True)
        p = jnp.exp(s - m)
        l = jnp.sum(p, axis=-1, keepdims=True)
        o2 = jnp.dot(p.astype(BF16), v_win, preferred_element_type=F32) / l
        o_ref[:, cols] = jnp.where(left, o2[0:NA_QB], o2[NA_QB:2 * NA_QB]).astype(o_ref.dtype)


def _na_attn(l, proj, rpb_flat):
    t = proj.shape[0]
    batch = t // SEQ
    n_qb = GRID_ROWS // NA_QROWS
    return pl.pallas_call(
        functools.partial(_na_kernel, l * NA_HEADS * RPB_ROWS * RPB_COLS),
        name="na_attn",
        grid=(batch, n_qb),
        in_specs=[
            pl.BlockSpec(memory_space=pltpu.SMEM),
            pl.BlockSpec((NA_QB, MIX_NA), lambda b, i: (b * n_qb + i, COL_QNA // MIX_NA)),
            pl.BlockSpec((SEQ, MIX_NA), lambda b, i: (b, COL_KNA // MIX_NA)),
            pl.BlockSpec((SEQ, MIX_NA), lambda b, i: (b, COL_VNA // MIX_NA)),
        ],
        out_specs=pl.BlockSpec((NA_QB, MIX_NA), lambda b, i: (b * n_qb + i, 0)),
        out_shape=jax.ShapeDtypeStruct((t, MIX_NA), BF16),
        scratch_shapes=[
            pltpu.VMEM((RPB_ROWS, GRID_W, LANES), F32),
            pltpu.VMEM((3, NA_HEADS * NA_QB, NA_KB), F32),
        ],
        compiler_params=_params("arbitrary", "arbitrary"),
    )(rpb_flat, proj, proj, proj)


def _out_proj_kernel(yc_ref, ym_ref, yn_ref, gc_ref, gm_ref, gn_ref, w_ref, x_ref, gp_ref,
                     o_ref, mix_ref):
    mix_ref[:, 0:CONV_CH] = _rms(yc_ref[...].astype(F32), gc_ref[...]).astype(BF16)
    mix_ref[:, CONV_CH:CONV_CH + MIX_MLA] = _rms(ym_ref[...].astype(F32), gm_ref[...]).astype(BF16)
    mix_ref[:, CONV_CH + MIX_MLA:] = _rms(yn_ref[...].astype(F32), gn_ref[...]).astype(BF16)
    f = jnp.dot(mix_ref[...], w_ref[...], preferred_element_type=F32)
    o_ref[...] = x_ref[...] + _rms(f, gp_ref[...])


def _out_proj(l, y_conv, y_mla, y_na, g_c, g_m, g_n, w_o, x2d, g_post):
    t = x2d.shape[0]
    mix_width = CONV_CH + MIX_MLA + MIX_NA

    def rows(width):
        return pl.BlockSpec((TM_OUT, width), lambda i: (i, 0))

    def vec(width):
        return _layer_vec(l, width)

    return pl.pallas_call(
        _out_proj_kernel,
        name="out_proj",
        grid=(t // TM_OUT,),
        in_specs=[
            rows(CONV_CH), rows(MIX_MLA), rows(MIX_NA),
            vec(CONV_CH), vec(MIX_MLA), vec(MIX_NA),
            pl.BlockSpec((None, mix_width, D_MODEL), lambda i: (l, 0, 0)),
            rows(D_MODEL), vec(D_MODEL),
        ],
        out_specs=rows(D_MODEL),
        out_shape=jax.ShapeDtypeStruct((t, D_MODEL), F32),
        scratch_shapes=[pltpu.VMEM((TM_OUT, mix_width), BF16)],
        compiler_params=_params("parallel"),
    )(y_conv, y_mla, y_na, g_c, g_m, g_n, w_o, x2d, g_post)


def _ffn_kernel(x_ref, gpre_ref, wup_ref, wdn_ref, gpost_ref, o_ref, h_ref, acc_ref):
    k = pl.program_id(1)

    @pl.when(k == 0)
    def _():
        h_ref[...] = _rms(x_ref[...], gpre_ref[...]).astype(BF16)

    a = jnp.maximum(jnp.dot(h_ref[...], wup_ref[...], preferred_element_type=F32), 0.0)
    part = jnp.dot((a * a).astype(BF16), wdn_ref[...], preferred_element_type=F32)

    @pl.when(k == 0)
    def _():
        acc_ref[...] = part

    @pl.when(k > 0)
    def _():
        acc_ref[...] += part

    @pl.when(k == pl.num_programs(1) - 1)
    def _():
        o_ref[...] = x_ref[...] + _rms(acc_ref[...], gpost_ref[...])


def _ffn(l, x2d, g_pre, w_up, w_down, g_post):
    t = x2d.shape[0]
    vec = _layer_vec(l, D_MODEL)
    return pl.pallas_call(
        _ffn_kernel,
        name="ffn",
        grid=(t // TM_FFN, D_FF // TK_FFN),
        in_specs=[
            pl.BlockSpec((TM_FFN, D_MODEL), lambda i, k: (i, 0)),
            vec,
            pl.BlockSpec((None, D_MODEL, TK_FFN), lambda i, k: (l, 0, k)),
            pl.BlockSpec((None, TK_FFN, D_MODEL), lambda i, k: (l, k, 0)),
            vec,
        ],
        out_specs=pl.BlockSpec((TM_FFN, D_MODEL), lambda i, k: (i, 0)),
        out_shape=jax.ShapeDtypeStruct((t, D_MODEL), F32),
        scratch_shapes=[pltpu.VMEM((TM_FFN, D_MODEL), BF16), pltpu.VMEM((TM_FFN, D_MODEL), F32)],
        compiler_params=_params("parallel", "arbitrary"),
    )(x2d, g_pre, w_up, w_down, g_post)


def _rotate_half_cols(w):
    half = MLA_ROPE // 2
    return jnp.concatenate([-w[..., half:], w[..., :half]], axis=-1)


def _prep_w_in(w_in):
    sizes = (2 * CONV_CH, MLA_Q_RANK, MLA_KV_RANK, MLA_ROPE, MIX_NA, MIX_NA, MIX_NA)
    starts = [sum(sizes[:n]) for n in range(len(sizes))]
    u_conv, c_q, c_kv, k_rope, q_na, k_na, v_na = (w_in[..., s:s + n] for s, n in zip(starts, sizes))
    parts = [u_conv, c_q, q_na, k_na, v_na, c_kv, k_rope, _rotate_half_cols(k_rope)]
    return jnp.concatenate(parts, axis=-1).astype(BF16)


def _prep_w_uq(w_uq):
    depth = w_uq.shape[0]
    w = w_uq.reshape(depth, MLA_Q_RANK, MLA_HEADS, MLA_NOPE + MLA_ROPE)
    pe = w[..., MLA_NOPE:]
    w = jnp.concatenate([w[..., :MLA_NOPE], pe, _rotate_half_cols(pe)], axis=-1)
    return w.transpose(0, 2, 1, 3).astype(BF16)


def _prep_w_ukv(w_ukv):
    depth = w_ukv.shape[0]
    w = w_ukv.reshape(depth, MLA_KV_RANK, MLA_HEADS, MLA_NOPE + MLA_V)
    return w.transpose(0, 2, 1, 3).astype(BF16)


def _rope_table():
    pos = jnp.arange(SEQ, dtype=F32)
    inv_freq = 1.0 / (ROPE_THETA ** (jnp.arange(0, MLA_ROPE, 2, dtype=F32) / MLA_ROPE))
    ang = pos[:, None] * inv_freq[None, :]
    cos, sin = jnp.cos(ang), jnp.sin(ang)
    return jnp.concatenate([cos, cos, sin, sin], axis=-1)


def _stack_rows(g):
    return g.reshape(g.shape[0], 1, g.shape[1])


def kernel(x, g_pre_mix, w_in, conv_w, conv_b, conv_ln_g, conv_ln_b, g_q_a, w_uq, g_kv_a, w_ukv,
           na_rpb, g_out_conv, g_out_mla, g_out_na, w_o, g_post_mix, g_pre_ffn, w_up, w_down,
           g_post_ffn):
    batch, seq, d_model = x.shape
    assert (seq, d_model) == (SEQ, D_MODEL)
    depth = w_in.shape[0]
    cs_table = _rope_table()
    w_in_ext = _prep_w_in(w_in)
    wq_heads = _prep_w_uq(w_uq)
    wkv_heads = _prep_w_ukv(w_ukv)
    w_o_b, w_up_b, w_down_b = w_o.astype(BF16), w_up.astype(BF16), w_down.astype(BF16)
    rpb_flat = na_rpb.reshape(-1)
    (g_pre_mix, conv_b, conv_ln_g, conv_ln_b, g_q_a, g_kv_a, g_out_conv, g_out_mla, g_out_na,
     g_post_mix, g_pre_ffn, g_post_ffn) = map(_stack_rows, (
         g_pre_mix, conv_b, conv_ln_g, conv_ln_b, g_q_a, g_kv_a, g_out_conv, g_out_mla, g_out_na,
         g_post_mix, g_pre_ffn, g_post_ffn))
    x2d = x.reshape(batch * seq, d_model)
    for l in range(depth):
        proj = _in_proj(l, x2d, g_pre_mix, w_in_ext)
        y_conv = _conv(l, proj, conv_w, conv_b, conv_ln_g, conv_ln_b)
        q, k, v = _mla_up(l, proj, cs_table, g_q_a, g_kv_a, wq_heads, wkv_heads)
        y_mla = _mla_attn(q, k, v)
        y_na = _na_attn(l, proj, rpb_flat)
        x2d = _out_proj(l, y_conv, y_mla, y_na, g_out_conv, g_out_mla, g_out_na, w_o_b, x2d, g_post_mix)
        x2d = _ffn(l, x2d, g_pre_ffn, w_up_b, w_down_b, g_post_ffn)
    return x2d.reshape(batch, seq, d_model)
```

```python
import functools

import jax
import jax.numpy as jnp
from jax import lax
from jax.experimental import pallas as pl
from jax.experimental.pallas import tpu as pltpu

D_MODEL = 2048
SEQ = 2048
GRID_W = 64
GRID_ROWS = SEQ // GRID_W
EPS = 1e-6
CONV_CH = 512
CONV_WIDTH = 31
CONV_PAD = (CONV_WIDTH - 1) // 2
MLA_HEADS = 8
MLA_Q_RANK = 512
MLA_KV_RANK = 256
MLA_NOPE = 128
MLA_ROPE = 64
MLA_V = 128
ROPE_THETA = 10000.0
NA_HEADS = 8
NA_HEAD_DIM = 64
NA_ROWS = 8
NA_COLS = 16
RPB_ROWS = 2 * NA_ROWS - 1
RPB_COLS = 2 * NA_COLS - 1
MIX_NA = NA_HEADS * NA_HEAD_DIM
MIX_MLA = MLA_HEADS * MLA_V
D_FF = 4 * D_MODEL

LANES = 128
SUBLANES = 8
VMEM_LIMIT_BYTES = 56 * 1024 * 1024

COL_CQ = 2 * CONV_CH
COL_QNA = COL_CQ + MLA_Q_RANK
COL_KNA = COL_QNA + MIX_NA
COL_VNA = COL_KNA + MIX_NA
COL_CKV = COL_VNA + MIX_NA
COL_KR = COL_CKV + MLA_KV_RANK
IN_COLS_EXT = COL_KR + 2 * MLA_ROPE
assert COL_CQ % MLA_Q_RANK == 0 and COL_CKV % MLA_KV_RANK == 0 and COL_KR % (2 * MLA_ROPE) == 0
assert COL_QNA % MIX_NA == 0 and COL_KNA % MIX_NA == 0 and COL_VNA % MIX_NA == 0

TM_IN = 1024
TN_IN = IN_COLS_EXT // 3
CONV_ROWS = 128
CONV_HALO = 16
TM_UP = 512
TQ_MLA = 512
MLA_HEADS_PER_STEP = 2
NA_QROWS = 4
NA_KROWS = 12
NA_QB = NA_QROWS * GRID_W
NA_KB = NA_KROWS * GRID_W
TM_OUT = 512
TM_FFN = 512
TK_FFN = 1024
MASK_VALUE = -1e30

F32 = jnp.float32
BF16 = jnp.bfloat16


def _params(*semantics):
    return pltpu.CompilerParams(dimension_semantics=semantics, vmem_limit_bytes=VMEM_LIMIT_BYTES)


def _layer_vec(l, width):
    return pl.BlockSpec((None, 1, width), lambda *_: (l, 0, 0))


def _rms(x, g):
    return x * lax.rsqrt(jnp.mean(x * x, axis=-1, keepdims=True) + EPS) * g


def _sigmoid(x):
    return 1.0 / (1.0 + jnp.exp(-x))


def _in_proj_kernel(x_ref, g_ref, w_ref, o_ref, h_ref):
    @pl.when(pl.program_id(1) == 0)
    def _():
        h_ref[...] = _rms(x_ref[...], g_ref[...]).astype(BF16)

    o_ref[...] = jnp.dot(h_ref[...], w_ref[...], preferred_element_type=F32).astype(o_ref.dtype)


def _in_proj(l, x2d, g, w_ext):
    t = x2d.shape[0]
    return pl.pallas_call(
        _in_proj_kernel,
        name="in_proj",
        grid=(t // TM_IN, IN_COLS_EXT // TN_IN),
        in_specs=[
            pl.BlockSpec((TM_IN, D_MODEL), lambda i, j: (i, 0)),
            _layer_vec(l, D_MODEL),
            pl.BlockSpec((None, D_MODEL, TN_IN), lambda i, j: (l, 0, j)),
        ],
        out_specs=pl.BlockSpec((TM_IN, TN_IN), lambda i, j: (i, j)),
        out_shape=jax.ShapeDtypeStruct((t, IN_COLS_EXT), BF16),
        scratch_shapes=[pltpu.VMEM((TM_IN, D_MODEL), BF16)],
        compiler_params=_params("parallel", "arbitrary"),
    )(x2d, g, w_ext)


def _conv_kernel(a_ref, gate_ref, w_ref, b_ref, lg_ref, lb_ref, o_ref, hpad_ref):
    n_tiles = SEQ // CONV_ROWS
    zeros = jnp.zeros((CONV_HALO, CONV_CH), F32)
    hpad_ref[0:CONV_HALO, :] = zeros
    hpad_ref[CONV_HALO + SEQ:2 * CONV_HALO + SEQ, :] = zeros

    def glu(t, carry):
        base = pl.multiple_of(t * CONV_ROWS, CONV_ROWS)
        a = a_ref[pl.ds(base, CONV_ROWS), :].astype(F32)
        gt = gate_ref[pl.ds(base, CONV_ROWS), :].astype(F32)
        hpad_ref[pl.ds(base + CONV_HALO, CONV_ROWS), :] = a * _sigmoid(gt)
        return carry

    lax.fori_loop(0, n_tiles, glu, 0)

    def tile(t, carry):
        base = pl.multiple_of(t * CONV_ROWS, CONV_ROWS)
        win_rows = CONV_ROWS + 2 * CONV_HALO
        accs = []
        for c in range(CONV_CH // LANES):
            cols = slice(c * LANES, (c + 1) * LANES)
            win = hpad_ref[pl.ds(base, win_rows), cols]
            acc = jnp.zeros((CONV_ROWS, LANES), F32)
            for r in range(SUBLANES):
                win_r = win if r == 0 else pltpu.roll(win, win_rows - r, axis=0)
                for q in range(2 * CONV_HALO // SUBLANES):
                    k = SUBLANES * q + r - (CONV_HALO - CONV_PAD)
                    if 0 <= k < CONV_WIDTH:
                        acc = acc + win_r[SUBLANES * q:SUBLANES * q + CONV_ROWS, :] * w_ref[k:k + 1, cols]
            accs.append(acc)
        acc = jnp.concatenate(accs, axis=1) + b_ref[...]
        mu = jnp.mean(acc, axis=-1, keepdims=True)
        xc = acc - mu
        var = jnp.mean(xc * xc, axis=-1, keepdims=True)
        y = xc * lax.rsqrt(var + EPS) * lg_ref[...] + lb_ref[...]
        o_ref[pl.ds(base, CONV_ROWS), :] = (y * _sigmoid(y)).astype(o_ref.dtype)
        return carry

    lax.fori_loop(0, n_tiles, tile, 0)


def _conv(l, proj, conv_w, conv_b, ln_g, ln_b):
    t = proj.shape[0]
    vec = _layer_vec(l, CONV_CH)
    return pl.pallas_call(
        _conv_kernel,
        name="conformer_conv",
        grid=(t // SEQ,),
        in_specs=[
            pl.BlockSpec((SEQ, CONV_CH), lambda b: (b, 0)),
            pl.BlockSpec((SEQ, CONV_CH), lambda b: (b, 1)),
            pl.BlockSpec((None, CONV_WIDTH, CONV_CH), lambda b: (l, 0, 0)),
            vec, vec, vec,
        ],
        out_specs=pl.BlockSpec((SEQ, CONV_CH), lambda b: (b, 0)),
        out_shape=jax.ShapeDtypeStruct((t, CONV_CH), BF16),
        scratch_shapes=[pltpu.VMEM((SEQ + 2 * CONV_HALO, CONV_CH), F32)],
        compiler_params=_params("parallel"),
    )(proj, proj, conv_w, conv_b, ln_g, ln_b)


def _rope_pair(block, cs):
    t = block * cs
    return t + pltpu.roll(t, MLA_ROPE, axis=1)


def _mla_up_kernel(cq_ref, ckv_ref, kr_ref, cs_ref, gq_ref, gkv_ref, wq_ref, wkv_ref,
                   q_ref, k_ref, v_ref):
    scale = (MLA_NOPE + MLA_ROPE) ** -0.5
    cs = cs_ref[...]
    hq = _rms(cq_ref[...].astype(F32), gq_ref[...]).astype(BF16)
    hkv = _rms(ckv_ref[...].astype(F32), gkv_ref[...]).astype(BF16)
    kr = _rope_pair(kr_ref[...].astype(F32), cs)
    lane = lax.broadcasted_iota(jnp.int32, kr.shape, 1)
    k_rot = jnp.where(lane < MLA_ROPE, kr, 0.0).astype(BF16)
    head_w = 2 * MLA_NOPE
    for h in range(MLA_HEADS):
        rq = jnp.dot(hq, wq_ref[:, h * head_w:(h + 1) * head_w], preferred_element_type=F32)
        q_ref[0, h, :, 0:MLA_NOPE] = (rq[:, 0:MLA_NOPE] * scale).astype(BF16)
        q_ref[0, h, :, MLA_NOPE:2 * MLA_NOPE] = (_rope_pair(rq[:, MLA_NOPE:2 * MLA_NOPE], cs) * scale).astype(BF16)
        rkv = jnp.dot(hkv, wkv_ref[:, h * head_w:(h + 1) * head_w], preferred_element_type=F32)
        k_ref[0, h, :, 0:MLA_NOPE] = rkv[:, 0:MLA_NOPE].astype(BF16)
        k_ref[0, h, :, MLA_NOPE:2 * MLA_NOPE] = k_rot
        v_ref[0, h] = rkv[:, MLA_NOPE:MLA_NOPE + MLA_V].astype(BF16)


def _mla_up(l, proj, cs_table, g_q, g_kv, wq_heads, wkv_heads):
    t = proj.shape[0]
    batch = t // SEQ
    n_s = SEQ // TM_UP
    qk_spec = pl.BlockSpec((1, MLA_HEADS, TM_UP, 2 * MLA_NOPE), lambda i: (i // n_s, 0, i % n_s, 0))
    return pl.pallas_call(
        _mla_up_kernel,
        name="mla_up",
        grid=(t // TM_UP,),
        in_specs=[
            pl.BlockSpec((TM_UP, MLA_Q_RANK), lambda i: (i, COL_CQ // MLA_Q_RANK)),
            pl.BlockSpec((TM_UP, MLA_KV_RANK), lambda i: (i, COL_CKV // MLA_KV_RANK)),
            pl.BlockSpec((TM_UP, 2 * MLA_ROPE), lambda i: (i, COL_KR // (2 * MLA_ROPE))),
            pl.BlockSpec((TM_UP, 2 * MLA_ROPE), lambda i: (i % n_s, 0)),
            _layer_vec(l, MLA_Q_RANK),
            _layer_vec(l, MLA_KV_RANK),
            pl.BlockSpec((None, MLA_Q_RANK, MLA_HEADS * 2 * MLA_NOPE), lambda i: (l, 0, 0)),
            pl.BlockSpec((None, MLA_KV_RANK, MLA_HEADS * (MLA_NOPE + MLA_V)), lambda i: (l, 0, 0)),
        ],
        out_specs=[
            qk_spec,
            qk_spec,
            pl.BlockSpec((1, MLA_HEADS, TM_UP, MLA_V), lambda i: (i // n_s, 0, i % n_s, 0)),
        ],
        out_shape=[
            jax.ShapeDtypeStruct((batch, MLA_HEADS, SEQ, 2 * MLA_NOPE), BF16),
            jax.ShapeDtypeStruct((batch, MLA_HEADS, SEQ, 2 * MLA_NOPE), BF16),
            jax.ShapeDtypeStruct((batch, MLA_HEADS, SEQ, MLA_V), BF16),
        ],
        compiler_params=_params("parallel"),
    )(proj, proj, proj, cs_table, g_q, g_kv, wq_heads, wkv_heads)


def _mla_attn_kernel(q_ref, k_ref, v_ref, o_ref):
    for c in range(SEQ // TQ_MLA):
        rows = slice(c * TQ_MLA, (c + 1) * TQ_MLA)
        for h in range(MLA_HEADS_PER_STEP):
            s = lax.dot_general(q_ref[0, h, rows, :], k_ref[0, h], (((1,), (1,)), ((), ())),
                                preferred_element_type=F32)
            m = jnp.max(s, axis=-1, keepdims=True)
            p = jnp.exp(s - m)
            l = jnp.sum(p, axis=-1, keepdims=True)
            o = jnp.dot(p.astype(BF16), v_ref[0, h], preferred_element_type=F32)
            o_ref[rows, h * MLA_V:(h + 1) * MLA_V] = (o / l).astype(o_ref.dtype)


def _mla_attn(q, k, v):
    batch = q.shape[0]
    qk_spec = pl.BlockSpec((1, MLA_HEADS_PER_STEP, SEQ, 2 * MLA_NOPE), lambda b, h: (b, h, 0, 0))
    return pl.pallas_call(
        _mla_attn_kernel,
        name="mla_attn",
        grid=(batch, MLA_HEADS // MLA_HEADS_PER_STEP),
        in_specs=[
            qk_spec,
            qk_spec,
            pl.BlockSpec((1, MLA_HEADS_PER_STEP, SEQ, MLA_V), lambda b, h: (b, h, 0, 0)),
        ],
        out_specs=pl.BlockSpec((SEQ, MLA_HEADS_PER_STEP * MLA_V), lambda b, h: (b, h)),
        out_shape=jax.ShapeDtypeStruct((batch * SEQ, MIX_MLA), BF16),
        compiler_params=_params("parallel", "parallel"),
    )(q, k, v)


def _na_pattern(pat, i, j):
    if pat == 0:
        return j <= NA_ROWS - 1, j - i + (NA_ROWS - 1)
    if pat == 1:
        return i <= j <= i + NA_ROWS - 1, j - i + (NA_ROWS - 1) - NA_QROWS
    return NA_KROWS - NA_ROWS <= j, j - i - 1


def _na_build_tables(rpb_base, rpb_ref, toep_ref, tab_ref):
    shape = (GRID_W, LANES)
    w_q = lax.broadcasted_iota(jnp.int32, shape, 0)
    lane = lax.broadcasted_iota(jnp.int32, shape, 1)
    w_k = lane & (GRID_W - 1)
    dc = w_k - w_q + (NA_COLS - 1)
    col_start = jnp.clip(w_q - NA_COLS // 2, 0, GRID_W - NA_COLS)
    col_valid = (w_k >= col_start) & (w_k < col_start + NA_COLS)
    left = lane < GRID_W
    neg = jnp.full(shape, MASK_VALUE, F32)

    def one_head(head, carry):
        for a in range(RPB_ROWS):
            acc = jnp.zeros(shape, F32)
            for bb in range(RPB_COLS):
                idx = rpb_base + head * (RPB_ROWS * RPB_COLS) + (a * RPB_COLS + bb)
                acc = jnp.where(dc == bb, rpb_ref[idx], acc)
            toep_ref[a] = acc
        for pat in range(3):
            for i in range(NA_QROWS):
                for jp in range(NA_KROWS // 2):
                    v_l, a_l = _na_pattern(pat, i, 2 * jp)
                    v_r, a_r = _na_pattern(pat, i, 2 * jp + 1)
                    if not (v_l or v_r):
                        tile = neg
                    else:
                        t_l = toep_ref[a_l] if v_l else neg
                        t_r = toep_ref[a_r] if v_r else neg
                        tile = jnp.where(col_valid, jnp.where(left, t_l, t_r), neg)
                    row0 = pl.multiple_of(head * NA_QB + i * GRID_W, GRID_W)
                    tab_ref[pat, pl.ds(row0, GRID_W), jp * LANES:(jp + 1) * LANES] = tile
        return carry

    lax.fori_loop(0, NA_HEADS, one_head, 0)


def _na_kernel(rpb_base, rpb_ref, q_ref, k_ref, v_ref, o_ref, toep_ref, tab_ref):
    qb = pl.program_id(1)
    n_qb = GRID_ROWS // NA_QROWS

    @pl.when((pl.program_id(0) == 0) & (qb == 0))
    def _():
        _na_build_tables(rpb_base, rpb_ref, toep_ref, tab_ref)

    start = pl.multiple_of(jnp.clip(qb - 1, 0, n_qb - NA_KROWS // NA_QROWS) * NA_QB, NA_QB)
    pat = jnp.where(qb == 0, 0, jnp.where(qb == n_qb - 1, 2, 1))
    lane = lax.broadcasted_iota(jnp.int32, (NA_QB, LANES), 1)
    left = lane < NA_HEAD_DIM
    for pair in range(NA_HEADS // 2):
        cols = slice(pair * LANES, (pair + 1) * LANES)
        k_win = k_ref[pl.ds(start, NA_KB), cols]
        v_win = v_ref[pl.ds(start, NA_KB), cols]
        q = q_ref[:, cols] * (NA_HEAD_DIM ** -0.5)
        zero = jnp.zeros_like(q)
        q2 = jnp.concatenate([jnp.where(left, q, zero), jnp.where(left, zero, q)], axis=0)
        s = lax.dot_general(q2, k_win, (((1,), (1,)), ((), ())), preferred_element_type=F32)
        s = s + tab_ref[pat, pair * 2 * NA_QB:(pair + 1) * 2 * NA_QB, :]
        m = jnp.max(s, axis=-1, keepdims=True)
        p = jnp.exp(s - m)
        l = jnp.sum(p, axis=-1, keepdims=True)
        o2 = jnp.dot(p.astype(BF16), v_win, preferred_element_type=F32) / l
        o_ref[:, cols] = jnp.where(left, o2[0:NA_QB], o2[NA_QB:2 * NA_QB]).astype(o_ref.dtype)


def _na_attn(l, proj, rpb_flat):
    t = proj.shape[0]
    batch = t // SEQ
    n_qb = GRID_ROWS // NA_QROWS
    return pl.pallas_call(
        functools.partial(_na_kernel, l * NA_HEADS * RPB_ROWS * RPB_COLS),
        name="na_attn",
        grid=(batch, n_qb),
        in_specs=[
            pl.BlockSpec(memory_space=pltpu.SMEM),
            pl.BlockSpec((NA_QB, MIX_NA), lambda b, i: (b * n_qb + i, COL_QNA // MIX_NA)),
            pl.BlockSpec((SEQ, MIX_NA), lambda b, i: (b, COL_KNA // MIX_NA)),
            pl.BlockSpec((SEQ, MIX_NA), lambda b, i: (b, COL_VNA // MIX_NA)),
        ],
        out_specs=pl.BlockSpec((NA_QB, MIX_NA), lambda b, i: (b * n_qb + i, 0)),
        out_shape=jax.ShapeDtypeStruct((t, MIX_NA), BF16),
        scratch_shapes=[
            pltpu.VMEM((RPB_ROWS, GRID_W, LANES), F32),
            pltpu.VMEM((3, NA_HEADS * NA_QB, NA_KB), F32),
        ],
        compiler_params=_params("arbitrary", "arbitrary"),
    )(rpb_flat, proj, proj, proj)


def _out_proj_kernel(yc_ref, ym_ref, yn_ref, gc_ref, gm_ref, gn_ref, w_ref, x_ref, gp_ref,
                     o_ref, mix_ref):
    mix_ref[:, 0:CONV_CH] = _rms(yc_ref[...].astype(F32), gc_ref[...]).astype(BF16)
    mix_ref[:, CONV_CH:CONV_CH + MIX_MLA] = _rms(ym_ref[...].astype(F32), gm_ref[...]).astype(BF16)
    mix_ref[:, CONV_CH + MIX_MLA:] = _rms(yn_ref[...].astype(F32), gn_ref[...]).astype(BF16)
    f = jnp.dot(mix_ref[...], w_ref[...], preferred_element_type=F32)
    o_ref[...] = x_ref[...] + _rms(f, gp_ref[...])


def _out_proj(l, y_conv, y_mla, y_na, g_c, g_m, g_n, w_o, x2d, g_post):
    t = x2d.shape[0]
    mix_width = CONV_CH + MIX_MLA + MIX_NA

    def rows(width):
        return pl.BlockSpec((TM_OUT, width), lambda i: (i, 0))

    def vec(width):
        return _layer_vec(l, width)

    return pl.pallas_call(
        _out_proj_kernel,
        name="out_proj",
        grid=(t // TM_OUT,),
        in_specs=[
            rows(CONV_CH), rows(MIX_MLA), rows(MIX_NA),
            vec(CONV_CH), vec(MIX_MLA), vec(MIX_NA),
            pl.BlockSpec((None, mix_width, D_MODEL), lambda i: (l, 0, 0)),
            rows(D_MODEL), vec(D_MODEL),
        ],
        out_specs=rows(D_MODEL),
        out_shape=jax.ShapeDtypeStruct((t, D_MODEL), F32),
        scratch_shapes=[pltpu.VMEM((TM_OUT, mix_width), BF16)],
        compiler_params=_params("parallel"),
    )(y_conv, y_mla, y_na, g_c, g_m, g_n, w_o, x2d, g_post)


def _ffn_kernel(x_ref, gpre_ref, wup_ref, wdn_ref, gpost_ref, o_ref, h_ref, acc_ref):
    k = pl.program_id(1)

    @pl.when(k == 0)
    def _():
        h_ref[...] = _rms(x_ref[...], gpre_ref[...]).astype(BF16)

    a = jnp.maximum(jnp.dot(h_ref[...], wup_ref[...], preferred_element_type=F32), 0.0)
    part = jnp.dot((a * a).astype(BF16), wdn_ref[...], preferred_element_type=F32)

    @pl.when(k == 0)
    def _():
        acc_ref[...] = part

    @pl.when(k > 0)
    def _():
        acc_ref[...] += part

    @pl.when(k == pl.num_programs(1) - 1)
    def _():
        o_ref[...] = x_ref[...] + _rms(acc_ref[...], gpost_ref[...])


def _ffn(l, x2d, g_pre, w_up, w_down, g_post):
    t = x2d.shape[0]
    vec = _layer_vec(l, D_MODEL)
    return pl.pallas_call(
        _ffn_kernel,
        name="ffn",
        grid=(t // TM_FFN, D_FF // TK_FFN),
        in_specs=[
            pl.BlockSpec((TM_FFN, D_MODEL), lambda i, k: (i, 0)),
            vec,
            pl.BlockSpec((None, D_MODEL, TK_FFN), lambda i, k: (l, 0, k)),
            pl.BlockSpec((None, TK_FFN, D_MODEL), lambda i, k: (l, k, 0)),
            vec,
        ],
        out_specs=pl.BlockSpec((TM_FFN, D_MODEL), lambda i, k: (i, 0)),
        out_shape=jax.ShapeDtypeStruct((t, D_MODEL), F32),
        scratch_shapes=[pltpu.VMEM((TM_FFN, D_MODEL), BF16), pltpu.VMEM((TM_FFN, D_MODEL), F32)],
        compiler_params=_params("parallel", "arbitrary"),
    )(x2d, g_pre, w_up, w_down, g_post)


def _rotate_half_cols(w):
    half = MLA_ROPE // 2
    return jnp.concatenate([-w[..., half:], w[..., :half]], axis=-1)


def _prep_w_in(w_in):
    sizes = (2 * CONV_CH, MLA_Q_RANK, MLA_KV_RANK, MLA_ROPE, MIX_NA, MIX_NA, MIX_NA)
    starts = [sum(sizes[:n]) for n in range(len(sizes))]
    w_in = w_in.astype(BF16)
    u_conv, c_q, c_kv, k_rope, q_na, k_na, v_na = (w_in[..., s:s + n] for s, n in zip(starts, sizes))
    parts = [u_conv, c_q, q_na, k_na, v_na, c_kv, k_rope, _rotate_half_cols(k_rope)]
    return jnp.concatenate(parts, axis=-1)


def _prep_w_uq(w_uq):
    depth = w_uq.shape[0]
    w = w_uq.astype(BF16).reshape(depth, MLA_Q_RANK, MLA_HEADS, MLA_NOPE + MLA_ROPE)
    pe = w[..., MLA_NOPE:]
    w = jnp.concatenate([w[..., :MLA_NOPE], pe, _rotate_half_cols(pe)], axis=-1)
    return w.reshape(depth, MLA_Q_RANK, MLA_HEADS * 2 * MLA_NOPE)


def _rope_table():
    pos = jnp.arange(SEQ, dtype=F32)
    inv_freq = 1.0 / (ROPE_THETA ** (jnp.arange(0, MLA_ROPE, 2, dtype=F32) / MLA_ROPE))
    ang = pos[:, None] * inv_freq[None, :]
    cos, sin = jnp.cos(ang), jnp.sin(ang)
    return jnp.concatenate([cos, cos, sin, sin], axis=-1)


def _stack_rows(g):
    return g.reshape(g.shape[0], 1, g.shape[1])


def kernel(x, g_pre_mix, w_in, conv_w, conv_b, conv_ln_g, conv_ln_b, g_q_a, w_uq, g_kv_a, w_ukv,
           na_rpb, g_out_conv, g_out_mla, g_out_na, w_o, g_post_mix, g_pre_ffn, w_up, w_down,
           g_post_ffn):
    batch, seq, d_model = x.shape
    assert (seq, d_model) == (SEQ, D_MODEL)
    depth = w_in.shape[0]
    cs_table = _rope_table()
    w_in_ext = _prep_w_in(w_in)
    wq_heads = _prep_w_uq(w_uq)
    wkv_heads = w_ukv.astype(BF16)
    w_o_b, w_up_b, w_down_b = w_o.astype(BF16), w_up.astype(BF16), w_down.astype(BF16)
    rpb_flat = na_rpb.reshape(-1)
    (g_pre_mix, conv_b, conv_ln_g, conv_ln_b, g_q_a, g_kv_a, g_out_conv, g_out_mla, g_out_na,
     g_post_mix, g_pre_ffn, g_post_ffn) = map(_stack_rows, (
         g_pre_mix, conv_b, conv_ln_g, conv_ln_b, g_q_a, g_kv_a, g_out_conv, g_out_mla, g_out_na,
         g_post_mix, g_pre_ffn, g_post_ffn))
    x2d = x.reshape(batch * seq, d_model)
    for l in range(depth):
        proj = _in_proj(l, x2d, g_pre_mix, w_in_ext)
        y_conv = _conv(l, proj, conv_w, conv_b, conv_ln_g, conv_ln_b)
        q, k, v = _mla_up(l, proj, cs_table, g_q_a, g_kv_a, wq_heads, wkv_heads)
        y_mla = _mla_attn(q, k, v)
        y_na = _na_attn(l, proj, rpb_flat)
        x2d = _out_proj(l, y_conv, y_mla, y_na, g_out_conv, g_out_mla, g_out_na, w_o_b, x2d, g_post_mix)
        x2d = _ffn(l, x2d, g_pre_ffn, w_up_b, w_down_b, g_post_ffn)
    return x2d.reshape(batch, seq, d_model)
```

```python
import functools

import jax
import jax.numpy as jnp
from jax import lax
from jax.experimental import pallas as pl
from jax.experimental.pallas import tpu as pltpu

D_MODEL = 2048
SEQ = 2048
GRID_W = 64
GRID_ROWS = SEQ // GRID_W
EPS = 1e-6
CONV_CH = 512
CONV_WIDTH = 31
CONV_PAD = (CONV_WIDTH - 1) // 2
MLA_HEADS = 8
MLA_Q_RANK = 512
MLA_KV_RANK = 256
MLA_NOPE = 128
MLA_ROPE = 64
MLA_V = 128
ROPE_THETA = 10000.0
NA_HEADS = 8
NA_HEAD_DIM = 64
NA_ROWS = 8
NA_COLS = 16
RPB_ROWS = 2 * NA_ROWS - 1
RPB_COLS = 2 * NA_COLS - 1
MIX_NA = NA_HEADS * NA_HEAD_DIM
MIX_MLA = MLA_HEADS * MLA_V
D_FF = 4 * D_MODEL

LANES = 128
SUBLANES = 8
VMEM_LIMIT_BYTES = 56 * 1024 * 1024

COL_CQ = 2 * CONV_CH
COL_QNA = COL_CQ + MLA_Q_RANK
COL_KNA = COL_QNA + MIX_NA
COL_VNA = COL_KNA + MIX_NA
COL_CKV = COL_VNA + MIX_NA
COL_KR = COL_CKV + MLA_KV_RANK
IN_COLS_EXT = COL_KR + 2 * MLA_ROPE
MXU_COLS = 256
IN_COLS_PAD = -(-IN_COLS_EXT // (2 * MXU_COLS)) * (2 * MXU_COLS)
assert COL_CQ % MLA_Q_RANK == 0 and COL_CKV % MLA_KV_RANK == 0 and COL_KR % (2 * MLA_ROPE) == 0
assert COL_QNA % MIX_NA == 0 and COL_KNA % MIX_NA == 0 and COL_VNA % MIX_NA == 0

TM_IN = 1024
TN_IN = IN_COLS_PAD // 2
CONV_ROWS = 128
CONV_HALO = 16
TM_UP = 512
TQ_MLA = 512
MLA_HEADS_PER_STEP = 2
NA_QROWS = 4
NA_KROWS = 12
NA_QB = NA_QROWS * GRID_W
NA_KB = NA_KROWS * GRID_W
TM_OUT = 512
TM_FFN = 512
TK_FFN = 2048
FFN_CHUNK = 1024
FFN_VMEM_LIMIT_BYTES = 60 * 1024 * 1024
MASK_VALUE = -1e30

F32 = jnp.float32
BF16 = jnp.bfloat16


def _params(*semantics):
    return pltpu.CompilerParams(dimension_semantics=semantics, vmem_limit_bytes=VMEM_LIMIT_BYTES)


def _layer_vec(l, width):
    return pl.BlockSpec((None, 1, width), lambda *_: (l, 0, 0))


def _rms(x, g):
    return x * lax.rsqrt(jnp.mean(x * x, axis=-1, keepdims=True) + EPS) * g


def _sigmoid(x):
    return 1.0 / (1.0 + jnp.exp(-x))


def _in_proj_kernel(x_ref, g_ref, w_ref, o_ref, h_ref):
    @pl.when(pl.program_id(1) == 0)
    def _():
        h_ref[...] = _rms(x_ref[...], g_ref[...]).astype(BF16)

    o_ref[...] = jnp.dot(h_ref[...], w_ref[...], preferred_element_type=F32).astype(o_ref.dtype)


def _in_proj(l, x2d, g, w_ext):
    t = x2d.shape[0]
    return pl.pallas_call(
        _in_proj_kernel,
        name="in_proj",
        grid=(t // TM_IN, IN_COLS_PAD // TN_IN),
        in_specs=[
            pl.BlockSpec((TM_IN, D_MODEL), lambda i, j: (i, 0)),
            _layer_vec(l, D_MODEL),
            pl.BlockSpec((None, D_MODEL, TN_IN), lambda i, j: (l, 0, j)),
        ],
        out_specs=pl.BlockSpec((TM_IN, TN_IN), lambda i, j: (i, j)),
        out_shape=jax.ShapeDtypeStruct((t, IN_COLS_PAD), BF16),
        scratch_shapes=[pltpu.VMEM((TM_IN, D_MODEL), BF16)],
        compiler_params=_params("parallel", "arbitrary"),
    )(x2d, g, w_ext)


def _conv_kernel(a_ref, gate_ref, w_ref, b_ref, lg_ref, lb_ref, o_ref, hpad_ref):
    n_tiles = SEQ // CONV_ROWS
    zeros = jnp.zeros((CONV_HALO, CONV_CH), F32)
    hpad_ref[0:CONV_HALO, :] = zeros
    hpad_ref[CONV_HALO + SEQ:2 * CONV_HALO + SEQ, :] = zeros

    def glu(t, carry):
        base = pl.multiple_of(t * CONV_ROWS, CONV_ROWS)
        a = a_ref[pl.ds(base, CONV_ROWS), :].astype(F32)
        gt = gate_ref[pl.ds(base, CONV_ROWS), :].astype(F32)
        hpad_ref[pl.ds(base + CONV_HALO, CONV_ROWS), :] = a * _sigmoid(gt)
        return carry

    lax.fori_loop(0, n_tiles, glu, 0)

    def tile(t, carry):
        base = pl.multiple_of(t * CONV_ROWS, CONV_ROWS)
        win_rows = CONV_ROWS + 2 * CONV_HALO
        accs = []
        for c in range(CONV_CH // LANES):
            cols = slice(c * LANES, (c + 1) * LANES)
            win = hpad_ref[pl.ds(base, win_rows), cols]
            acc = jnp.zeros((CONV_ROWS, LANES), F32)
            for r in range(SUBLANES):
                win_r = win if r == 0 else pltpu.roll(win, win_rows - r, axis=0)
                for q in range(2 * CONV_HALO // SUBLANES):
                    k = SUBLANES * q + r - (CONV_HALO - CONV_PAD)
                    if 0 <= k < CONV_WIDTH:
                        acc = acc + win_r[SUBLANES * q:SUBLANES * q + CONV_ROWS, :] * w_ref[k:k + 1, cols]
            accs.append(acc)
        acc = jnp.concatenate(accs, axis=1) + b_ref[...]
        mu = jnp.mean(acc, axis=-1, keepdims=True)
        xc = acc - mu
        var = jnp.mean(xc * xc, axis=-1, keepdims=True)
        y = xc * lax.rsqrt(var + EPS) * lg_ref[...] + lb_ref[...]
        o_ref[pl.ds(base, CONV_ROWS), :] = (y * _sigmoid(y)).astype(o_ref.dtype)
        return carry

    lax.fori_loop(0, n_tiles, tile, 0)


def _conv(l, proj, conv_w, conv_b, ln_g, ln_b):
    t = proj.shape[0]
    vec = _layer_vec(l, CONV_CH)
    return pl.pallas_call(
        _conv_kernel,
        name="conformer_conv",
        grid=(t // SEQ,),
        in_specs=[
            pl.BlockSpec((SEQ, CONV_CH), lambda b: (b, 0)),
            pl.BlockSpec((SEQ, CONV_CH), lambda b: (b, 1)),
            pl.BlockSpec((None, CONV_WIDTH, CONV_CH), lambda b: (l, 0, 0)),
            vec, vec, vec,
        ],
        out_specs=pl.BlockSpec((SEQ, CONV_CH), lambda b: (b, 0)),
        out_shape=jax.ShapeDtypeStruct((t, CONV_CH), BF16),
        scratch_shapes=[pltpu.VMEM((SEQ + 2 * CONV_HALO, CONV_CH), F32)],
        compiler_params=_params("parallel"),
    )(proj, proj, conv_w, conv_b, ln_g, ln_b)


def _rope_pair(block, cs):
    t = block * cs
    return t + pltpu.roll(t, MLA_ROPE, axis=1)


def _mla_up_kernel(cq_ref, ckv_ref, kr_ref, cs_ref, gq_ref, gkv_ref, wq_ref, wkv_ref,
                   q_ref, k_ref, v_ref):
    scale = (MLA_NOPE + MLA_ROPE) ** -0.5
    cs = cs_ref[...]
    hq = _rms(cq_ref[...].astype(F32), gq_ref[...]).astype(BF16)
    hkv = _rms(ckv_ref[...].astype(F32), gkv_ref[...]).astype(BF16)
    kr = _rope_pair(kr_ref[...].astype(F32), cs)
    lane = lax.broadcasted_iota(jnp.int32, kr.shape, 1)
    k_rot = jnp.where(lane < MLA_ROPE, kr, 0.0).astype(BF16)
    head_w = 2 * MLA_NOPE
    for h in range(MLA_HEADS):
        rq = jnp.dot(hq, wq_ref[:, h * head_w:(h + 1) * head_w], preferred_element_type=F32)
        q_ref[0, h, :, 0:MLA_NOPE] = (rq[:, 0:MLA_NOPE] * scale).astype(BF16)
        q_ref[0, h, :, MLA_NOPE:2 * MLA_NOPE] = (_rope_pair(rq[:, MLA_NOPE:2 * MLA_NOPE], cs) * scale).astype(BF16)
        rkv = jnp.dot(hkv, wkv_ref[:, h * head_w:(h + 1) * head_w], preferred_element_type=F32)
        k_ref[0, h, :, 0:MLA_NOPE] = rkv[:, 0:MLA_NOPE].astype(BF16)
        k_ref[0, h, :, MLA_NOPE:2 * MLA_NOPE] = k_rot
        v_ref[0, h] = rkv[:, MLA_NOPE:MLA_NOPE + MLA_V].astype(BF16)


def _mla_up(l, proj, cs_table, g_q, g_kv, wq_heads, wkv_heads):
    t = proj.shape[0]
    batch = t // SEQ
    n_s = SEQ // TM_UP
    qk_spec = pl.BlockSpec((1, MLA_HEADS, TM_UP, 2 * MLA_NOPE), lambda i: (i // n_s, 0, i % n_s, 0))
    return pl.pallas_call(
        _mla_up_kernel,
        name="mla_up",
        grid=(t // TM_UP,),
        in_specs=[
            pl.BlockSpec((TM_UP, MLA_Q_RANK), lambda i: (i, COL_CQ // MLA_Q_RANK)),
            pl.BlockSpec((TM_UP, MLA_KV_RANK), lambda i: (i, COL_CKV // MLA_KV_RANK)),
            pl.BlockSpec((TM_UP, 2 * MLA_ROPE), lambda i: (i, COL_KR // (2 * MLA_ROPE))),
            pl.BlockSpec((TM_UP, 2 * MLA_ROPE), lambda i: (i % n_s, 0)),
            _layer_vec(l, MLA_Q_RANK),
            _layer_vec(l, MLA_KV_RANK),
            pl.BlockSpec((None, MLA_Q_RANK, MLA_HEADS * 2 * MLA_NOPE), lambda i: (l, 0, 0)),
            pl.BlockSpec((None, MLA_KV_RANK, MLA_HEADS * (MLA_NOPE + MLA_V)), lambda i: (l, 0, 0)),
        ],
        out_specs=[
            qk_spec,
            qk_spec,
            pl.BlockSpec((1, MLA_HEADS, TM_UP, MLA_V), lambda i: (i // n_s, 0, i % n_s, 0)),
        ],
        out_shape=[
            jax.ShapeDtypeStruct((batch, MLA_HEADS, SEQ, 2 * MLA_NOPE), BF16),
            jax.ShapeDtypeStruct((batch, MLA_HEADS, SEQ, 2 * MLA_NOPE), BF16),
            jax.ShapeDtypeStruct((batch, MLA_HEADS, SEQ, MLA_V), BF16),
        ],
        compiler_params=_params("parallel"),
    )(proj, proj, proj, cs_table, g_q, g_kv, wq_heads, wkv_heads)


def _mla_attn_kernel(q_ref, k_ref, v_ref, o_ref):
    for c in range(SEQ // TQ_MLA):
        rows = slice(c * TQ_MLA, (c + 1) * TQ_MLA)
        for h in range(MLA_HEADS_PER_STEP):
            s = lax.dot_general(q_ref[0, h, rows, :], k_ref[0, h], (((1,), (1,)), ((), ())),
                                preferred_element_type=F32)
            m = jnp.max(s, axis=-1, keepdims=True)
            p = jnp.exp(s - m)
            l = jnp.sum(p, axis=-1, keepdims=True)
            o = jnp.dot(p.astype(BF16), v_ref[0, h], preferred_element_type=F32)
            o_ref[rows, h * MLA_V:(h + 1) * MLA_V] = (o / l).astype(o_ref.dtype)


def _mla_attn(q, k, v):
    batch = q.shape[0]
    qk_spec = pl.BlockSpec((1, MLA_HEADS_PER_STEP, SEQ, 2 * MLA_NOPE), lambda b, h: (b, h, 0, 0))
    return pl.pallas_call(
        _mla_attn_kernel,
        name="mla_attn",
        grid=(batch, MLA_HEADS // MLA_HEADS_PER_STEP),
        in_specs=[
            qk_spec,
            qk_spec,
            pl.BlockSpec((1, MLA_HEADS_PER_STEP, SEQ, MLA_V), lambda b, h: (b, h, 0, 0)),
        ],
        out_specs=pl.BlockSpec((SEQ, MLA_HEADS_PER_STEP * MLA_V), lambda b, h: (b, h)),
        out_shape=jax.ShapeDtypeStruct((batch * SEQ, MIX_MLA), BF16),
        compiler_params=_params("parallel", "parallel"),
    )(q, k, v)


def _na_pattern(pat, i, j):
    if pat == 0:
        return j <= NA_ROWS - 1, j - i + (NA_ROWS - 1)
    if pat == 1:
        return i <= j <= i + NA_ROWS - 1, j - i + (NA_ROWS - 1) - NA_QROWS
    return NA_KROWS - NA_ROWS <= j, j - i - 1


def _na_build_tables(rpb_base, rpb_ref, toep_ref, tab_ref):
    shape = (GRID_W, LANES)
    w_q = lax.broadcasted_iota(jnp.int32, shape, 0)
    lane = lax.broadcasted_iota(jnp.int32, shape, 1)
    w_k = lane & (GRID_W - 1)
    dc = w_k - w_q + (NA_COLS - 1)
    col_start = jnp.clip(w_q - NA_COLS // 2, 0, GRID_W - NA_COLS)
    col_valid = (w_k >= col_start) & (w_k < col_start + NA_COLS)
    left = lane < GRID_W
    neg = jnp.full(shape, MASK_VALUE, F32)

    def one_head(head, carry):
        for a in range(RPB_ROWS):
            acc = jnp.zeros(shape, F32)
            for bb in range(RPB_COLS):
                idx = rpb_base + head * (RPB_ROWS * RPB_COLS) + (a * RPB_COLS + bb)
                acc = jnp.where(dc == bb, rpb_ref[idx], acc)
            toep_ref[a] = acc
        for pat in range(3):
            for i in range(NA_QROWS):
                for jp in range(NA_KROWS // 2):
                    v_l, a_l = _na_pattern(pat, i, 2 * jp)
                    v_r, a_r = _na_pattern(pat, i, 2 * jp + 1)
                    if not (v_l or v_r):
                        tile = neg
                    else:
                        t_l = toep_ref[a_l] if v_l else neg
                        t_r = toep_ref[a_r] if v_r else neg
                        tile = jnp.where(col_valid, jnp.where(left, t_l, t_r), neg)
                    row0 = pl.multiple_of(head * NA_QB + i * GRID_W, GRID_W)
                    tab_ref[pat, pl.ds(row0, GRID_W), jp * LANES:(jp + 1) * LANES] = tile
        return carry

    lax.fori_loop(0, NA_HEADS, one_head, 0)


def _na_kernel(rpb_base, rpb_ref, q_ref, k_ref, v_ref, o_ref, toep_ref, tab_ref):
    qb = pl.program_id(1)
    n_qb = GRID_ROWS // NA_QROWS

    @pl.when((pl.program_id(0) == 0) & (qb == 0))
    def _():
        _na_build_tables(rpb_base, rpb_ref, toep_ref, tab_ref)

    start = pl.multiple_of(jnp.clip(qb - 1, 0, n_qb - NA_KROWS // NA_QROWS) * NA_QB, NA_QB)
    pat = jnp.where(qb == 0, 0, jnp.where(qb == n_qb - 1, 2, 1))
    lane = lax.broadcasted_iota(jnp.int32, (NA_QB, LANES), 1)
    left = lane < NA_HEAD_DIM
    for pair in range(NA_HEADS // 2):
        cols = slice(pair * LANES, (pair + 1) * LANES)
        k_win = k_ref[pl.ds(start, NA_KB), cols]
        v_win = v_ref[pl.ds(start, NA_KB), cols]
        q = q_ref[:, cols] * (NA_HEAD_DIM ** -0.5)
        zero = jnp.zeros_like(q)
        q2 = jnp.concatenate([jnp.where(left, q, zero), jnp.where(left, zero, q)], axis=0)
        s = lax.dot_general(q2, k_win, (((1,), (1,)), ((), ())), preferred_element_type=F32)
        s = s + tab_ref[pat, pair * 2 * NA_QB:(pair + 1) * 2 * NA_QB, :]
        m = jnp.max(s, axis=-1, keepdims=True)
        p = jnp.exp(s - m)
        l = jnp.sum(p, axis=-1, keepdims=True)
        o2 = jnp.dot(p.astype(BF16), v_win, preferred_element_type=F32) / l
        o_ref[:, cols] = jnp.where(left, o2[0:NA_QB], o2[NA_QB:2 * NA_QB]).astype(o_ref.dtype)


def _na_attn(l, proj, rpb_flat):
    t = proj.shape[0]
    batch = t // SEQ
    n_qb = GRID_ROWS // NA_QROWS
    return pl.pallas_call(
        functools.partial(_na_kernel, l * NA_HEADS * RPB_ROWS * RPB_COLS),
        name="na_attn",
        grid=(batch, n_qb),
        in_specs=[
            pl.BlockSpec(memory_space=pltpu.SMEM),
            pl.BlockSpec((NA_QB, MIX_NA), lambda b, i: (b * n_qb + i, COL_QNA // MIX_NA)),
            pl.BlockSpec((SEQ, MIX_NA), lambda b, i: (b, COL_KNA // MIX_NA)),
            pl.BlockSpec((SEQ, MIX_NA), lambda b, i: (b, COL_VNA // MIX_NA)),
        ],
        out_specs=pl.BlockSpec((NA_QB, MIX_NA), lambda b, i: (b * n_qb + i, 0)),
        out_shape=jax.ShapeDtypeStruct((t, MIX_NA), BF16),
        scratch_shapes=[
            pltpu.VMEM((RPB_ROWS, GRID_W, LANES), F32),
            pltpu.VMEM((3, NA_HEADS * NA_QB, NA_KB), F32),
        ],
        compiler_params=_params("arbitrary", "arbitrary"),
    )(rpb_flat, proj, proj, proj)


def _out_proj_kernel(yc_ref, ym_ref, yn_ref, gc_ref, gm_ref, gn_ref, w_ref, x_ref, gp_ref,
                     o_ref, mix_ref):
    mix_ref[:, 0:CONV_CH] = _rms(yc_ref[...].astype(F32), gc_ref[...]).astype(BF16)
    mix_ref[:, CONV_CH:CONV_CH + MIX_MLA] = _rms(ym_ref[...].astype(F32), gm_ref[...]).astype(BF16)
    mix_ref[:, CONV_CH + MIX_MLA:] = _rms(yn_ref[...].astype(F32), gn_ref[...]).astype(BF16)
    f = jnp.dot(mix_ref[...], w_ref[...], preferred_element_type=F32)
    o_ref[...] = x_ref[...] + _rms(f, gp_ref[...])


def _out_proj(l, y_conv, y_mla, y_na, g_c, g_m, g_n, w_o, x2d, g_post):
    t = x2d.shape[0]
    mix_width = CONV_CH + MIX_MLA + MIX_NA

    def rows(width):
        return pl.BlockSpec((TM_OUT, width), lambda i: (i, 0))

    def vec(width):
        return _layer_vec(l, width)

    return pl.pallas_call(
        _out_proj_kernel,
        name="out_proj",
        grid=(t // TM_OUT,),
        in_specs=[
            rows(CONV_CH), rows(MIX_MLA), rows(MIX_NA),
            vec(CONV_CH), vec(MIX_MLA), vec(MIX_NA),
            pl.BlockSpec((None, mix_width, D_MODEL), lambda i: (l, 0, 0)),
            rows(D_MODEL), vec(D_MODEL),
        ],
        out_specs=rows(D_MODEL),
        out_shape=jax.ShapeDtypeStruct((t, D_MODEL), F32),
        scratch_shapes=[pltpu.VMEM((TM_OUT, mix_width), BF16)],
        compiler_params=_params("parallel"),
    )(y_conv, y_mla, y_na, g_c, g_m, g_n, w_o, x2d, g_post)


def _ffn_kernel(x_ref, gpre_ref, wup_ref, wdn_ref, gpost_ref, o_ref, h_ref):
    k = pl.program_id(1)
    last = pl.num_programs(1) - 1

    @pl.when(k == 0)
    def _():
        h_ref[...] = _rms(x_ref[...], gpre_ref[...]).astype(BF16)

    part = None
    for c in range(TK_FFN // FFN_CHUNK):
        cols = slice(c * FFN_CHUNK, (c + 1) * FFN_CHUNK)
        a = jnp.maximum(jnp.dot(h_ref[...], wup_ref[:, cols], preferred_element_type=F32), 0.0)
        d = jnp.dot((a * a).astype(BF16), wdn_ref[cols, :], preferred_element_type=F32)
        part = d if part is None else part + d

    @pl.when(k == 0)
    def _():
        o_ref[...] = part

    @pl.when((k > 0) & (k < last))
    def _():
        o_ref[...] += part

    @pl.when(k == last)
    def _():
        o_ref[...] = x_ref[...] + _rms(o_ref[...] + part, gpost_ref[...])


def _ffn(l, x2d, g_pre, w_up, w_down, g_post):
    t = x2d.shape[0]
    vec = _layer_vec(l, D_MODEL)
    return pl.pallas_call(
        _ffn_kernel,
        name="ffn",
        grid=(t // TM_FFN, D_FF // TK_FFN),
        in_specs=[
            pl.BlockSpec((TM_FFN, D_MODEL), lambda i, k: (i, 0)),
            vec,
            pl.BlockSpec((None, D_MODEL, TK_FFN), lambda i, k: (l, 0, k)),
            pl.BlockSpec((None, TK_FFN, D_MODEL), lambda i, k: (l, k, 0)),
            vec,
        ],
        out_specs=pl.BlockSpec((TM_FFN, D_MODEL), lambda i, k: (i, 0)),
        out_shape=jax.ShapeDtypeStruct((t, D_MODEL), F32),
        scratch_shapes=[pltpu.VMEM((TM_FFN, D_MODEL), BF16)],
        compiler_params=pltpu.CompilerParams(dimension_semantics=("parallel", "arbitrary"),
                                             vmem_limit_bytes=FFN_VMEM_LIMIT_BYTES),
    )(x2d, g_pre, w_up, w_down, g_post)


def _rotate_half_cols(w):
    half = MLA_ROPE // 2
    return jnp.concatenate([-w[..., half:], w[..., :half]], axis=-1)


def _prep_w_in(w_in):
    sizes = (2 * CONV_CH, MLA_Q_RANK, MLA_KV_RANK, MLA_ROPE, MIX_NA, MIX_NA, MIX_NA)
    starts = [sum(sizes[:n]) for n in range(len(sizes))]
    w_in = w_in.astype(BF16)
    u_conv, c_q, c_kv, k_rope, q_na, k_na, v_na = (w_in[..., s:s + n] for s, n in zip(starts, sizes))
    pad = jnp.zeros(w_in.shape[:-1] + (IN_COLS_PAD - IN_COLS_EXT,), BF16)
    parts = [u_conv, c_q, q_na, k_na, v_na, c_kv, k_rope, _rotate_half_cols(k_rope), pad]
    return jnp.concatenate(parts, axis=-1)


def _prep_w_uq(w_uq):
    depth = w_uq.shape[0]
    w = w_uq.astype(BF16).reshape(depth, MLA_Q_RANK, MLA_HEADS, MLA_NOPE + MLA_ROPE)
    pe = w[..., MLA_NOPE:]
    w = jnp.concatenate([w[..., :MLA_NOPE], pe, _rotate_half_cols(pe)], axis=-1)
    return w.reshape(depth, MLA_Q_RANK, MLA_HEADS * 2 * MLA_NOPE)


def _rope_table():
    pos = jnp.arange(SEQ, dtype=F32)
    inv_freq = 1.0 / (ROPE_THETA ** (jnp.arange(0, MLA_ROPE, 2, dtype=F32) / MLA_ROPE))
    ang = pos[:, None] * inv_freq[None, :]
    cos, sin = jnp.cos(ang), jnp.sin(ang)
    return jnp.concatenate([cos, cos, sin, sin], axis=-1)


def _stack_rows(g):
    return g.reshape(g.shape[0], 1, g.shape[1])


def kernel(x, g_pre_mix, w_in, conv_w, conv_b, conv_ln_g, conv_ln_b, g_q_a, w_uq, g_kv_a, w_ukv,
           na_rpb, g_out_conv, g_out_mla, g_out_na, w_o, g_post_mix, g_pre_ffn, w_up, w_down,
           g_post_ffn):
    batch, seq, d_model = x.shape
    assert (seq, d_model) == (SEQ, D_MODEL)
    depth = w_in.shape[0]
    cs_table = _rope_table()
    w_in_ext = _prep_w_in(w_in)
    wq_heads = _prep_w_uq(w_uq)
    wkv_heads = w_ukv.astype(BF16)
    w_o_b, w_up_b, w_down_b = w_o.astype(BF16), w_up.astype(BF16), w_down.astype(BF16)
    rpb_flat = na_rpb.reshape(-1)
    (g_pre_mix, conv_b, conv_ln_g, conv_ln_b, g_q_a, g_kv_a, g_out_conv, g_out_mla, g_out_na,
     g_post_mix, g_pre_ffn, g_post_ffn) = map(_stack_rows, (
         g_pre_mix, conv_b, conv_ln_g, conv_ln_b, g_q_a, g_kv_a, g_out_conv, g_out_mla, g_out_na,
         g_post_mix, g_pre_ffn, g_post_ffn))
    x2d = x.reshape(batch * seq, d_model)
    for l in range(depth):
        proj = _in_proj(l, x2d, g_pre_mix, w_in_ext)
        y_conv = _conv(l, proj, conv_w, conv_b, conv_ln_g, conv_ln_b)
        q, k, v = _mla_up(l, proj, cs_table, g_q_a, g_kv_a, wq_heads, wkv_heads)
        y_mla = _mla_attn(q, k, v)
        y_na = _na_attn(l, proj, rpb_flat)
        x2d = _out_proj(l, y_conv, y_mla, y_na, g_out_conv, g_out_mla, g_out_na, w_o_b, x2d, g_post_mix)
        x2d = _ffn(l, x2d, g_pre_ffn, w_up_b, w_down_b, g_post_ffn)
    return x2d.reshape(batch, seq, d_model)
```

```python
import functools

import jax
import jax.numpy as jnp
from jax import lax
from jax.experimental import pallas as pl
from jax.experimental.pallas import tpu as pltpu

D_MODEL = 2048
SEQ = 2048
GRID_W = 64
GRID_ROWS = SEQ // GRID_W
EPS = 1e-6
CONV_CH = 512
CONV_WIDTH = 31
CONV_PAD = (CONV_WIDTH - 1) // 2
MLA_HEADS = 8
MLA_Q_RANK = 512
MLA_KV_RANK = 256
MLA_NOPE = 128
MLA_ROPE = 64
MLA_V = 128
ROPE_THETA = 10000.0
NA_HEADS = 8
NA_HEAD_DIM = 64
NA_ROWS = 8
NA_COLS = 16
RPB_ROWS = 2 * NA_ROWS - 1
RPB_COLS = 2 * NA_COLS - 1
MIX_NA = NA_HEADS * NA_HEAD_DIM
MIX_MLA = MLA_HEADS * MLA_V
D_FF = 4 * D_MODEL

LANES = 128
SUBLANES = 8
VMEM_LIMIT_BYTES = 56 * 1024 * 1024

COL_CQ = 2 * CONV_CH
COL_QNA = COL_CQ + MLA_Q_RANK
COL_KNA = COL_QNA + MIX_NA
COL_VNA = COL_KNA + MIX_NA
COL_CKV = COL_VNA + MIX_NA
COL_KR = COL_CKV + MLA_KV_RANK
IN_COLS_EXT = COL_KR + 2 * MLA_ROPE
MXU_COLS = 256
IN_COLS_PAD = -(-IN_COLS_EXT // (2 * MXU_COLS)) * (2 * MXU_COLS)
assert COL_CQ % MLA_Q_RANK == 0 and COL_CKV % MLA_KV_RANK == 0 and COL_KR % (2 * MLA_ROPE) == 0
assert COL_QNA % MIX_NA == 0 and COL_KNA % MIX_NA == 0 and COL_VNA % MIX_NA == 0

TM_IN = 1024
TN_IN = IN_COLS_PAD // 2
CONV_ROWS = 128
CONV_HALO = 16
TM_UP = 512
TQ_MLA = 512
MLA_HEADS_PER_STEP = 2
NA_QROWS = 4
NA_KROWS = 12
NA_QB_PER_STEP = 2
NA_QB = NA_QROWS * GRID_W
NA_KB = NA_KROWS * GRID_W
TM_OUT = 512
TM_FFN = 512
TK_FFN = 2048
FFN_CHUNK = 1024
FFN_VMEM_LIMIT_BYTES = 60 * 1024 * 1024
MASK_VALUE = -1e30

F32 = jnp.float32
BF16 = jnp.bfloat16


def _params(*semantics):
    return pltpu.CompilerParams(dimension_semantics=semantics, vmem_limit_bytes=VMEM_LIMIT_BYTES)


def _layer_vec(l, width):
    return pl.BlockSpec((None, 1, width), lambda *_: (l, 0, 0))


def _rms(x, g):
    return x * lax.rsqrt(jnp.mean(x * x, axis=-1, keepdims=True) + EPS) * g


def _sigmoid(x):
    return 1.0 / (1.0 + jnp.exp(-x))


def _in_proj_kernel(x_ref, g_ref, w_ref, o_ref, h_ref):
    @pl.when(pl.program_id(1) == 0)
    def _():
        h_ref[...] = _rms(x_ref[...], g_ref[...]).astype(BF16)

    o_ref[...] = jnp.dot(h_ref[...], w_ref[...], preferred_element_type=F32).astype(o_ref.dtype)


def _in_proj(l, x2d, g, w_ext):
    t = x2d.shape[0]
    return pl.pallas_call(
        _in_proj_kernel,
        name="in_proj",
        grid=(t // TM_IN, IN_COLS_PAD // TN_IN),
        in_specs=[
            pl.BlockSpec((TM_IN, D_MODEL), lambda i, j: (i, 0)),
            _layer_vec(l, D_MODEL),
            pl.BlockSpec((None, D_MODEL, TN_IN), lambda i, j: (l, 0, j)),
        ],
        out_specs=pl.BlockSpec((TM_IN, TN_IN), lambda i, j: (i, j)),
        out_shape=jax.ShapeDtypeStruct((t, IN_COLS_PAD), BF16),
        scratch_shapes=[pltpu.VMEM((TM_IN, D_MODEL), BF16)],
        compiler_params=_params("parallel", "arbitrary"),
    )(x2d, g, w_ext)


def _conv_kernel(a_ref, gate_ref, w_ref, b_ref, lg_ref, lb_ref, o_ref, hpad_ref):
    n_tiles = SEQ // CONV_ROWS
    zeros = jnp.zeros((CONV_HALO, CONV_CH), F32)
    hpad_ref[0:CONV_HALO, :] = zeros
    hpad_ref[CONV_HALO + SEQ:2 * CONV_HALO + SEQ, :] = zeros

    def glu(t, carry):
        base = pl.multiple_of(t * CONV_ROWS, CONV_ROWS)
        a = a_ref[pl.ds(base, CONV_ROWS), :].astype(F32)
        gt = gate_ref[pl.ds(base, CONV_ROWS), :].astype(F32)
        hpad_ref[pl.ds(base + CONV_HALO, CONV_ROWS), :] = a * _sigmoid(gt)
        return carry

    lax.fori_loop(0, n_tiles, glu, 0)

    def tile(t, carry):
        base = pl.multiple_of(t * CONV_ROWS, CONV_ROWS)
        win_rows = CONV_ROWS + 2 * CONV_HALO
        accs = []
        for c in range(CONV_CH // LANES):
            cols = slice(c * LANES, (c + 1) * LANES)
            win = hpad_ref[pl.ds(base, win_rows), cols]
            acc = jnp.zeros((CONV_ROWS, LANES), F32)
            for r in range(SUBLANES):
                win_r = win if r == 0 else pltpu.roll(win, win_rows - r, axis=0)
                for q in range(2 * CONV_HALO // SUBLANES):
                    k = SUBLANES * q + r - (CONV_HALO - CONV_PAD)
                    if 0 <= k < CONV_WIDTH:
                        acc = acc + win_r[SUBLANES * q:SUBLANES * q + CONV_ROWS, :] * w_ref[k:k + 1, cols]
            accs.append(acc)
        acc = jnp.concatenate(accs, axis=1) + b_ref[...]
        mu = jnp.mean(acc, axis=-1, keepdims=True)
        xc = acc - mu
        var = jnp.mean(xc * xc, axis=-1, keepdims=True)
        y = xc * lax.rsqrt(var + EPS) * lg_ref[...] + lb_ref[...]
        o_ref[pl.ds(base, CONV_ROWS), :] = (y * _sigmoid(y)).astype(o_ref.dtype)
        return carry

    lax.fori_loop(0, n_tiles, tile, 0)


def _conv(l, proj, conv_w, conv_b, ln_g, ln_b):
    t = proj.shape[0]
    vec = _layer_vec(l, CONV_CH)
    return pl.pallas_call(
        _conv_kernel,
        name="conformer_conv",
        grid=(t // SEQ,),
        in_specs=[
            pl.BlockSpec((SEQ, CONV_CH), lambda b: (b, 0)),
            pl.BlockSpec((SEQ, CONV_CH), lambda b: (b, 1)),
            pl.BlockSpec((None, CONV_WIDTH, CONV_CH), lambda b: (l, 0, 0)),
            vec, vec, vec,
        ],
        out_specs=pl.BlockSpec((SEQ, CONV_CH), lambda b: (b, 0)),
        out_shape=jax.ShapeDtypeStruct((t, CONV_CH), BF16),
        scratch_shapes=[pltpu.VMEM((SEQ + 2 * CONV_HALO, CONV_CH), F32)],
        compiler_params=_params("parallel"),
    )(proj, proj, conv_w, conv_b, ln_g, ln_b)


def _rope_pair(block, cs):
    t = block * cs
    return t + pltpu.roll(t, MLA_ROPE, axis=1)


def _mla_up_kernel(cq_ref, ckv_ref, kr_ref, cs_ref, gq_ref, gkv_ref, wq_ref, wkv_ref,
                   q_ref, k_ref, v_ref):
    scale = (MLA_NOPE + MLA_ROPE) ** -0.5
    cs = cs_ref[...]
    hq = _rms(cq_ref[...].astype(F32), gq_ref[...]).astype(BF16)
    hkv = _rms(ckv_ref[...].astype(F32), gkv_ref[...]).astype(BF16)
    kr = _rope_pair(kr_ref[...].astype(F32), cs)
    lane = lax.broadcasted_iota(jnp.int32, kr.shape, 1)
    k_rot = jnp.where(lane < MLA_ROPE, kr, 0.0).astype(BF16)
    head_w = 2 * MLA_NOPE
    for h in range(MLA_HEADS):
        rq = jnp.dot(hq, wq_ref[:, h * head_w:(h + 1) * head_w], preferred_element_type=F32)
        q_ref[0, h, :, 0:MLA_NOPE] = (rq[:, 0:MLA_NOPE] * scale).astype(BF16)
        q_ref[0, h, :, MLA_NOPE:2 * MLA_NOPE] = (_rope_pair(rq[:, MLA_NOPE:2 * MLA_NOPE], cs) * scale).astype(BF16)
        rkv = jnp.dot(hkv, wkv_ref[:, h * head_w:(h + 1) * head_w], preferred_element_type=F32)
        k_ref[0, h, :, 0:MLA_NOPE] = rkv[:, 0:MLA_NOPE].astype(BF16)
        k_ref[0, h, :, MLA_NOPE:2 * MLA_NOPE] = k_rot
        v_ref[0, h] = rkv[:, MLA_NOPE:MLA_NOPE + MLA_V].astype(BF16)


def _mla_up(l, proj, cs_table, g_q, g_kv, wq_heads, wkv_heads):
    t = proj.shape[0]
    batch = t // SEQ
    n_s = SEQ // TM_UP
    qk_spec = pl.BlockSpec((1, MLA_HEADS, TM_UP, 2 * MLA_NOPE), lambda i: (i // n_s, 0, i % n_s, 0))
    return pl.pallas_call(
        _mla_up_kernel,
        name="mla_up",
        grid=(t // TM_UP,),
        in_specs=[
            pl.BlockSpec((TM_UP, MLA_Q_RANK), lambda i: (i, COL_CQ // MLA_Q_RANK)),
            pl.BlockSpec((TM_UP, MLA_KV_RANK), lambda i: (i, COL_CKV // MLA_KV_RANK)),
            pl.BlockSpec((TM_UP, 2 * MLA_ROPE), lambda i: (i, COL_KR // (2 * MLA_ROPE))),
            pl.BlockSpec((TM_UP, 2 * MLA_ROPE), lambda i: (i % n_s, 0)),
            _layer_vec(l, MLA_Q_RANK),
            _layer_vec(l, MLA_KV_RANK),
            pl.BlockSpec((None, MLA_Q_RANK, MLA_HEADS * 2 * MLA_NOPE), lambda i: (l, 0, 0)),
            pl.BlockSpec((None, MLA_KV_RANK, MLA_HEADS * (MLA_NOPE + MLA_V)), lambda i: (l, 0, 0)),
        ],
        out_specs=[
            qk_spec,
            qk_spec,
            pl.BlockSpec((1, MLA_HEADS, TM_UP, MLA_V), lambda i: (i // n_s, 0, i % n_s, 0)),
        ],
        out_shape=[
            jax.ShapeDtypeStruct((batch, MLA_HEADS, SEQ, 2 * MLA_NOPE), BF16),
            jax.ShapeDtypeStruct((batch, MLA_HEADS, SEQ, 2 * MLA_NOPE), BF16),
            jax.ShapeDtypeStruct((batch, MLA_HEADS, SEQ, MLA_V), BF16),
        ],
        compiler_params=_params("parallel"),
    )(proj, proj, proj, cs_table, g_q, g_kv, wq_heads, wkv_heads)


def _mla_attn_kernel(q_ref, k_ref, v_ref, o_ref):
    for c in range(SEQ // TQ_MLA):
        rows = slice(c * TQ_MLA, (c + 1) * TQ_MLA)
        for h in range(MLA_HEADS_PER_STEP):
            s = lax.dot_general(q_ref[0, h, rows, :], k_ref[0, h], (((1,), (1,)), ((), ())),
                                preferred_element_type=F32)
            m = jnp.max(s, axis=-1, keepdims=True)
            p = jnp.exp(s - m)
            l = jnp.sum(p, axis=-1, keepdims=True)
            o = jnp.dot(p.astype(BF16), v_ref[0, h], preferred_element_type=F32)
            o_ref[rows, h * MLA_V:(h + 1) * MLA_V] = (o / l).astype(o_ref.dtype)


def _mla_attn(q, k, v):
    batch = q.shape[0]
    qk_spec = pl.BlockSpec((1, MLA_HEADS_PER_STEP, SEQ, 2 * MLA_NOPE), lambda b, h: (b, h, 0, 0))
    return pl.pallas_call(
        _mla_attn_kernel,
        name="mla_attn",
        grid=(batch, MLA_HEADS // MLA_HEADS_PER_STEP),
        in_specs=[
            qk_spec,
            qk_spec,
            pl.BlockSpec((1, MLA_HEADS_PER_STEP, SEQ, MLA_V), lambda b, h: (b, h, 0, 0)),
        ],
        out_specs=pl.BlockSpec((SEQ, MLA_HEADS_PER_STEP * MLA_V), lambda b, h: (b, h)),
        out_shape=jax.ShapeDtypeStruct((batch * SEQ, MIX_MLA), BF16),
        compiler_params=_params("parallel", "parallel"),
    )(q, k, v)


def _na_pattern(pat, i, j):
    if pat == 0:
        return j <= NA_ROWS - 1, j - i + (NA_ROWS - 1)
    if pat == 1:
        return i <= j <= i + NA_ROWS - 1, j - i + (NA_ROWS - 1) - NA_QROWS
    return NA_KROWS - NA_ROWS <= j, j - i - 1


def _na_build_tables(rpb_base, rpb_ref, toep_ref, tab_ref):
    shape = (GRID_W, LANES)
    w_q = lax.broadcasted_iota(jnp.int32, shape, 0)
    lane = lax.broadcasted_iota(jnp.int32, shape, 1)
    w_k = lane & (GRID_W - 1)
    dc = w_k - w_q + (NA_COLS - 1)
    col_start = jnp.clip(w_q - NA_COLS // 2, 0, GRID_W - NA_COLS)
    col_valid = (w_k >= col_start) & (w_k < col_start + NA_COLS)
    left = lane < GRID_W
    neg = jnp.full(shape, MASK_VALUE, F32)

    def one_head(head, carry):
        for a in range(RPB_ROWS):
            acc = jnp.zeros(shape, F32)
            for bb in range(RPB_COLS):
                idx = rpb_base + head * (RPB_ROWS * RPB_COLS) + (a * RPB_COLS + bb)
                acc = jnp.where(dc == bb, rpb_ref[idx], acc)
            toep_ref[a] = acc
        for pat in range(3):
            for i in range(NA_QROWS):
                for jp in range(NA_KROWS // 2):
                    v_l, a_l = _na_pattern(pat, i, 2 * jp)
                    v_r, a_r = _na_pattern(pat, i, 2 * jp + 1)
                    if not (v_l or v_r):
                        tile = neg
                    else:
                        t_l = toep_ref[a_l] if v_l else neg
                        t_r = toep_ref[a_r] if v_r else neg
                        tile = jnp.where(col_valid, jnp.where(left, t_l, t_r), neg)
                    row0 = pl.multiple_of(head * NA_QB + i * GRID_W, GRID_W)
                    tab_ref[pat, pl.ds(row0, GRID_W), jp * LANES:(jp + 1) * LANES] = tile
        return carry

    lax.fori_loop(0, NA_HEADS, one_head, 0)


def _na_kernel(rpb_base, rpb_ref, q_ref, k_ref, v_ref, o_ref, toep_ref, tab_ref):
    step = pl.program_id(1)
    n_qb = GRID_ROWS // NA_QROWS

    @pl.when((pl.program_id(0) == 0) & (step == 0))
    def _():
        _na_build_tables(rpb_base, rpb_ref, toep_ref, tab_ref)

    lane = lax.broadcasted_iota(jnp.int32, (NA_QB, LANES), 1)
    left = lane < NA_HEAD_DIM
    for sub in range(NA_QB_PER_STEP):
        qb = step * NA_QB_PER_STEP + sub
        rows = slice(sub * NA_QB, (sub + 1) * NA_QB)
        start = pl.multiple_of(jnp.clip(qb - 1, 0, n_qb - NA_KROWS // NA_QROWS) * NA_QB, NA_QB)
        pat = jnp.where(qb == 0, 0, jnp.where(qb == n_qb - 1, 2, 1))
        for pair in range(NA_HEADS // 2):
            cols = slice(pair * LANES, (pair + 1) * LANES)
            k_win = k_ref[pl.ds(start, NA_KB), cols]
            v_win = v_ref[pl.ds(start, NA_KB), cols]
            q = q_ref[rows, cols] * (NA_HEAD_DIM ** -0.5)
            zero = jnp.zeros_like(q)
            q2 = jnp.concatenate([jnp.where(left, q, zero), jnp.where(left, zero, q)], axis=0)
            s = lax.dot_general(q2, k_win, (((1,), (1,)), ((), ())), preferred_element_type=F32)
            s = s + tab_ref[pat, pair * 2 * NA_QB:(pair + 1) * 2 * NA_QB, :]
            m = jnp.max(s, axis=-1, keepdims=True)
            p = jnp.exp(s - m)
            l = jnp.sum(p, axis=-1, keepdims=True)
            o2 = jnp.dot(p.astype(BF16), v_win, preferred_element_type=F32) / l
            o_ref[rows, cols] = jnp.where(left, o2[0:NA_QB], o2[NA_QB:2 * NA_QB]).astype(o_ref.dtype)


def _na_attn(l, proj, rpb_flat):
    t = proj.shape[0]
    batch = t // SEQ
    n_steps = GRID_ROWS // NA_QROWS // NA_QB_PER_STEP
    step_rows = NA_QB_PER_STEP * NA_QB
    return pl.pallas_call(
        functools.partial(_na_kernel, l * NA_HEADS * RPB_ROWS * RPB_COLS),
        name="na_attn",
        grid=(batch, n_steps),
        in_specs=[
            pl.BlockSpec(memory_space=pltpu.SMEM),
            pl.BlockSpec((step_rows, MIX_NA), lambda b, i: (b * n_steps + i, COL_QNA // MIX_NA)),
            pl.BlockSpec((SEQ, MIX_NA), lambda b, i: (b, COL_KNA // MIX_NA)),
            pl.BlockSpec((SEQ, MIX_NA), lambda b, i: (b, COL_VNA // MIX_NA)),
        ],
        out_specs=pl.BlockSpec((step_rows, MIX_NA), lambda b, i: (b * n_steps + i, 0)),
        out_shape=jax.ShapeDtypeStruct((t, MIX_NA), BF16),
        scratch_shapes=[
            pltpu.VMEM((RPB_ROWS, GRID_W, LANES), F32),
            pltpu.VMEM((3, NA_HEADS * NA_QB, NA_KB), F32),
        ],
        compiler_params=_params("arbitrary", "arbitrary"),
    )(rpb_flat, proj, proj, proj)


def _out_proj_kernel(yc_ref, ym_ref, yn_ref, gc_ref, gm_ref, gn_ref, w_ref, x_ref, gp_ref,
                     o_ref, mix_ref):
    mix_ref[:, 0:CONV_CH] = _rms(yc_ref[...].astype(F32), gc_ref[...]).astype(BF16)
    mix_ref[:, CONV_CH:CONV_CH + MIX_MLA] = _rms(ym_ref[...].astype(F32), gm_ref[...]).astype(BF16)
    mix_ref[:, CONV_CH + MIX_MLA:] = _rms(yn_ref[...].astype(F32), gn_ref[...]).astype(BF16)
    f = jnp.dot(mix_ref[...], w_ref[...], preferred_element_type=F32)
    o_ref[...] = x_ref[...] + _rms(f, gp_ref[...])


def _out_proj(l, y_conv, y_mla, y_na, g_c, g_m, g_n, w_o, x2d, g_post):
    t = x2d.shape[0]
    mix_width = CONV_CH + MIX_MLA + MIX_NA

    def rows(width):
        return pl.BlockSpec((TM_OUT, width), lambda i: (i, 0))

    def vec(width):
        return _layer_vec(l, width)

    return pl.pallas_call(
        _out_proj_kernel,
        name="out_proj",
        grid=(t // TM_OUT,),
        in_specs=[
            rows(CONV_CH), rows(MIX_MLA), rows(MIX_NA),
            vec(CONV_CH), vec(MIX_MLA), vec(MIX_NA),
            pl.BlockSpec((None, mix_width, D_MODEL), lambda i: (l, 0, 0)),
            rows(D_MODEL), vec(D_MODEL),
        ],
        out_specs=rows(D_MODEL),
        out_shape=jax.ShapeDtypeStruct((t, D_MODEL), F32),
        scratch_shapes=[pltpu.VMEM((TM_OUT, mix_width), BF16)],
        compiler_params=_params("parallel"),
    )(y_conv, y_mla, y_na, g_c, g_m, g_n, w_o, x2d, g_post)


def _ffn_kernel(x_ref, gpre_ref, wup_ref, wdn_ref, gpost_ref, o_ref, h_ref):
    k = pl.program_id(1)
    last = pl.num_programs(1) - 1

    @pl.when(k == 0)
    def _():
        h_ref[...] = _rms(x_ref[...], gpre_ref[...]).astype(BF16)

    part = None
    for c in range(TK_FFN // FFN_CHUNK):
        cols = slice(c * FFN_CHUNK, (c + 1) * FFN_CHUNK)
        a = jnp.maximum(jnp.dot(h_ref[...], wup_ref[:, cols], preferred_element_type=F32), 0.0)
        d = jnp.dot((a * a).astype(BF16), wdn_ref[cols, :], preferred_element_type=F32)
        part = d if part is None else part + d

    @pl.when(k == 0)
    def _():
        o_ref[...] = part

    @pl.when((k > 0) & (k < last))
    def _():
        o_ref[...] += part

    @pl.when(k == last)
    def _():
        o_ref[...] = x_ref[...] + _rms(o_ref[...] + part, gpost_ref[...])


def _ffn(l, x2d, g_pre, w_up, w_down, g_post):
    t = x2d.shape[0]
    vec = _layer_vec(l, D_MODEL)
    return pl.pallas_call(
        _ffn_kernel,
        name="ffn",
        grid=(t // TM_FFN, D_FF // TK_FFN),
        in_specs=[
            pl.BlockSpec((TM_FFN, D_MODEL), lambda i, k: (i, 0)),
            vec,
            pl.BlockSpec((None, D_MODEL, TK_FFN), lambda i, k: (l, 0, k)),
            pl.BlockSpec((None, TK_FFN, D_MODEL), lambda i, k: (l, k, 0)),
            vec,
        ],
        out_specs=pl.BlockSpec((TM_FFN, D_MODEL), lambda i, k: (i, 0)),
        out_shape=jax.ShapeDtypeStruct((t, D_MODEL), F32),
        scratch_shapes=[pltpu.VMEM((TM_FFN, D_MODEL), BF16)],
        compiler_params=pltpu.CompilerParams(dimension_semantics=("parallel", "arbitrary"),
                                             vmem_limit_bytes=FFN_VMEM_LIMIT_BYTES),
    )(x2d, g_pre, w_up, w_down, g_post)


def _rotate_half_cols(w):
    half = MLA_ROPE // 2
    return jnp.concatenate([-w[..., half:], w[..., :half]], axis=-1)


def _prep_w_in(w_in):
    sizes = (2 * CONV_CH, MLA_Q_RANK, MLA_KV_RANK, MLA_ROPE, MIX_NA, MIX_NA, MIX_NA)
    starts = [sum(sizes[:n]) for n in range(len(sizes))]
    u_conv, c_q, c_kv, k_rope, q_na, k_na, v_na = (w_in[..., s:s + n] for s, n in zip(starts, sizes))
    pad = jnp.zeros(w_in.shape[:-1] + (IN_COLS_PAD - IN_COLS_EXT,), w_in.dtype)
    parts = [u_conv, c_q, q_na, k_na, v_na, c_kv, k_rope, _rotate_half_cols(k_rope), pad]
    return jnp.concatenate(parts, axis=-1).astype(BF16)


def _prep_w_uq(w_uq):
    depth = w_uq.shape[0]
    w = w_uq.astype(BF16).reshape(depth, MLA_Q_RANK, MLA_HEADS, MLA_NOPE + MLA_ROPE)
    pe = w[..., MLA_NOPE:]
    w = jnp.concatenate([w[..., :MLA_NOPE], pe, _rotate_half_cols(pe)], axis=-1)
    return w.reshape(depth, MLA_Q_RANK, MLA_HEADS * 2 * MLA_NOPE)


def _rope_table():
    pos = jnp.arange(SEQ, dtype=F32)
    inv_freq = 1.0 / (ROPE_THETA ** (jnp.arange(0, MLA_ROPE, 2, dtype=F32) / MLA_ROPE))
    ang = pos[:, None] * inv_freq[None, :]
    cos, sin = jnp.cos(ang), jnp.sin(ang)
    return jnp.concatenate([cos, cos, sin, sin], axis=-1)


def _stack_rows(g):
    return g.reshape(g.shape[0], 1, g.shape[1])


def kernel(x, g_pre_mix, w_in, conv_w, conv_b, conv_ln_g, conv_ln_b, g_q_a, w_uq, g_kv_a, w_ukv,
           na_rpb, g_out_conv, g_out_mla, g_out_na, w_o, g_post_mix, g_pre_ffn, w_up, w_down,
           g_post_ffn):
    batch, seq, d_model = x.shape
    assert (seq, d_model) == (SEQ, D_MODEL)
    depth = w_in.shape[0]
    cs_table = _rope_table()
    w_in_ext = _prep_w_in(w_in)
    wq_heads = _prep_w_uq(w_uq)
    wkv_heads = w_ukv.astype(BF16)
    w_o_b, w_up_b, w_down_b = w_o.astype(BF16), w_up.astype(BF16), w_down.astype(BF16)
    rpb_flat = na_rpb.reshape(-1)
    (g_pre_mix, conv_b, conv_ln_g, conv_ln_b, g_q_a, g_kv_a, g_out_conv, g_out_mla, g_out_na,
     g_post_mix, g_pre_ffn, g_post_ffn) = map(_stack_rows, (
         g_pre_mix, conv_b, conv_ln_g, conv_ln_b, g_q_a, g_kv_a, g_out_conv, g_out_mla, g_out_na,
         g_post_mix, g_pre_ffn, g_post_ffn))
    x2d = x.reshape(batch * seq, d_model)
    for l in range(depth):
        proj = _in_proj(l, x2d, g_pre_mix, w_in_ext)
        y_conv = _conv(l, proj, conv_w, conv_b, conv_ln_g, conv_ln_b)
        q, k, v = _mla_up(l, proj, cs_table, g_q_a, g_kv_a, wq_heads, wkv_heads)
        y_mla = _mla_attn(q, k, v)
        y_na = _na_attn(l, proj, rpb_flat)
        x2d = _out_proj(l, y_conv, y_mla, y_na, g_out_conv, g_out_mla, g_out_na, w_o_b, x2d, g_post_mix)
        x2d = _ffn(l, x2d, g_pre_ffn, w_up_b, w_down_b, g_post_ffn)
    return x2d.reshape(batch, seq, d_model)
```

```python
import functools

import jax
import jax.numpy as jnp
from jax import lax
from jax.experimental import pallas as pl
from jax.experimental.pallas import tpu as pltpu

D_MODEL = 2048
SEQ = 2048
GRID_W = 64
GRID_ROWS = SEQ // GRID_W
EPS = 1e-6
CONV_CH = 512
CONV_WIDTH = 31
CONV_PAD = (CONV_WIDTH - 1) // 2
MLA_HEADS = 8
MLA_Q_RANK = 512
MLA_KV_RANK = 256
MLA_NOPE = 128
MLA_ROPE = 64
MLA_V = 128
ROPE_THETA = 10000.0
NA_HEADS = 8
NA_HEAD_DIM = 64
NA_ROWS = 8
NA_COLS = 16
RPB_ROWS = 2 * NA_ROWS - 1
RPB_COLS = 2 * NA_COLS - 1
MIX_NA = NA_HEADS * NA_HEAD_DIM
MIX_MLA = MLA_HEADS * MLA_V
D_FF = 4 * D_MODEL

LANES = 128
SUBLANES = 8
VMEM_LIMIT_BYTES = 56 * 1024 * 1024

COL_CQ = 2 * CONV_CH
COL_QNA = COL_CQ + MLA_Q_RANK
COL_KNA = COL_QNA + MIX_NA
COL_VNA = COL_KNA + MIX_NA
COL_CKV = COL_VNA + MIX_NA
COL_KR = COL_CKV + MLA_KV_RANK
IN_COLS_EXT = COL_KR + 2 * MLA_ROPE
MXU_COLS = 256
IN_COLS_PAD = -(-IN_COLS_EXT // (2 * MXU_COLS)) * (2 * MXU_COLS)
assert COL_CQ % MLA_Q_RANK == 0 and COL_CKV % MLA_KV_RANK == 0 and COL_KR % (2 * MLA_ROPE) == 0
assert COL_QNA % MIX_NA == 0 and COL_KNA % MIX_NA == 0 and COL_VNA % MIX_NA == 0

TM_IN = 1024
TN_IN = IN_COLS_PAD // 2
CONV_ROWS = 128
CONV_HALO = 16
TM_UP = 512
TQ_MLA = 512
MLA_HEADS_PER_STEP = 2
NA_QROWS = 4
NA_KROWS = 12
NA_QB_PER_STEP = 2
NA_QB = NA_QROWS * GRID_W
NA_KB = NA_KROWS * GRID_W
TM_OUT = 512
TM_FFN = 512
TK_FFN = 2048
FFN_CHUNK = 1024
FFN_VMEM_LIMIT_BYTES = 60 * 1024 * 1024
MASK_VALUE = -1e30

F32 = jnp.float32
BF16 = jnp.bfloat16


def _params(*semantics):
    return pltpu.CompilerParams(dimension_semantics=semantics, vmem_limit_bytes=VMEM_LIMIT_BYTES)


def _layer_vec(l, width):
    return pl.BlockSpec((None, 1, width), lambda *_: (l, 0, 0))


def _rms(x, g):
    return x * lax.rsqrt(jnp.mean(x * x, axis=-1, keepdims=True) + EPS) * g


def _sigmoid(x):
    return 1.0 / (1.0 + jnp.exp(-x))


def _in_proj_kernel(x_ref, g_ref, w_ref, o_ref, h_ref):
    @pl.when(pl.program_id(1) == 0)
    def _():
        h_ref[...] = _rms(x_ref[...], g_ref[...]).astype(BF16)

    o_ref[...] = jnp.dot(h_ref[...], w_ref[...], preferred_element_type=F32).astype(o_ref.dtype)


def _in_proj(l, x2d, g, w_ext):
    t = x2d.shape[0]
    return pl.pallas_call(
        _in_proj_kernel,
        name="in_proj",
        grid=(t // TM_IN, IN_COLS_PAD // TN_IN),
        in_specs=[
            pl.BlockSpec((TM_IN, D_MODEL), lambda i, j: (i, 0)),
            _layer_vec(l, D_MODEL),
            pl.BlockSpec((None, D_MODEL, TN_IN), lambda i, j: (l, 0, j)),
        ],
        out_specs=pl.BlockSpec((TM_IN, TN_IN), lambda i, j: (i, j)),
        out_shape=jax.ShapeDtypeStruct((t, IN_COLS_PAD), BF16),
        scratch_shapes=[pltpu.VMEM((TM_IN, D_MODEL), BF16)],
        compiler_params=_params("parallel", "arbitrary"),
    )(x2d, g, w_ext)


def _conv_kernel(a_ref, gate_ref, w_ref, b_ref, lg_ref, lb_ref, o_ref, hpad_ref):
    n_tiles = SEQ // CONV_ROWS
    zeros = jnp.zeros((CONV_HALO, CONV_CH), F32)
    hpad_ref[0:CONV_HALO, :] = zeros
    hpad_ref[CONV_HALO + SEQ:2 * CONV_HALO + SEQ, :] = zeros

    def glu(t, carry):
        base = pl.multiple_of(t * CONV_ROWS, CONV_ROWS)
        a = a_ref[pl.ds(base, CONV_ROWS), :].astype(F32)
        gt = gate_ref[pl.ds(base, CONV_ROWS), :].astype(F32)
        hpad_ref[pl.ds(base + CONV_HALO, CONV_ROWS), :] = a * _sigmoid(gt)
        return carry

    lax.fori_loop(0, n_tiles, glu, 0)

    def tile(t, carry):
        base = pl.multiple_of(t * CONV_ROWS, CONV_ROWS)
        win_rows = CONV_ROWS + 2 * CONV_HALO
        accs = []
        for c in range(CONV_CH // LANES):
            cols = slice(c * LANES, (c + 1) * LANES)
            win = hpad_ref[pl.ds(base, win_rows), cols]
            acc = jnp.zeros((CONV_ROWS, LANES), F32)
            for r in range(SUBLANES):
                win_r = win if r == 0 else pltpu.roll(win, win_rows - r, axis=0)
                for q in range(2 * CONV_HALO // SUBLANES):
                    k = SUBLANES * q + r - (CONV_HALO - CONV_PAD)
                    if 0 <= k < CONV_WIDTH:
                        acc = acc + win_r[SUBLANES * q:SUBLANES * q + CONV_ROWS, :] * w_ref[k:k + 1, cols]
            accs.append(acc)
        acc = jnp.concatenate(accs, axis=1) + b_ref[...]
        mu = jnp.mean(acc, axis=-1, keepdims=True)
        xc = acc - mu
        var = jnp.mean(xc * xc, axis=-1, keepdims=True)
        y = xc * lax.rsqrt(var + EPS) * lg_ref[...] + lb_ref[...]
        o_ref[pl.ds(base, CONV_ROWS), :] = (y * _sigmoid(y)).astype(o_ref.dtype)
        return carry

    lax.fori_loop(0, n_tiles, tile, 0)


def _conv(l, proj, conv_w, conv_b, ln_g, ln_b):
    t = proj.shape[0]
    vec = _layer_vec(l, CONV_CH)
    return pl.pallas_call(
        _conv_kernel,
        name="conformer_conv",
        grid=(t // SEQ,),
        in_specs=[
            pl.BlockSpec((SEQ, CONV_CH), lambda b: (b, 0)),
            pl.BlockSpec((SEQ, CONV_CH), lambda b: (b, 1)),
            pl.BlockSpec((None, CONV_WIDTH, CONV_CH), lambda b: (l, 0, 0)),
            vec, vec, vec,
        ],
        out_specs=pl.BlockSpec((SEQ, CONV_CH), lambda b: (b, 0)),
        out_shape=jax.ShapeDtypeStruct((t, CONV_CH), BF16),
        scratch_shapes=[pltpu.VMEM((SEQ + 2 * CONV_HALO, CONV_CH), F32)],
        compiler_params=_params("parallel"),
    )(proj, proj, conv_w, conv_b, ln_g, ln_b)


def _rope_pair(block, cs):
    t = block * cs
    return t + pltpu.roll(t, MLA_ROPE, axis=1)


def _mla_up_kernel(cq_ref, ckv_ref, kr_ref, cs_ref, gq_ref, gkv_ref, wq_ref, wkv_ref,
                   q_ref, k_ref, v_ref):
    scale = (MLA_NOPE + MLA_ROPE) ** -0.5
    cs = cs_ref[...]
    hq = _rms(cq_ref[...].astype(F32), gq_ref[...]).astype(BF16)
    hkv = _rms(ckv_ref[...].astype(F32), gkv_ref[...]).astype(BF16)
    kr = _rope_pair(kr_ref[...].astype(F32), cs)
    lane = lax.broadcasted_iota(jnp.int32, kr.shape, 1)
    k_rot = jnp.where(lane < MLA_ROPE, kr, 0.0).astype(BF16)
    head_w = 2 * MLA_NOPE
    for h in range(MLA_HEADS):
        rq = jnp.dot(hq, wq_ref[:, h * head_w:(h + 1) * head_w], preferred_element_type=F32)
        q_ref[0, h, :, 0:MLA_NOPE] = (rq[:, 0:MLA_NOPE] * scale).astype(BF16)
        q_ref[0, h, :, MLA_NOPE:2 * MLA_NOPE] = (_rope_pair(rq[:, MLA_NOPE:2 * MLA_NOPE], cs) * scale).astype(BF16)
        rkv = jnp.dot(hkv, wkv_ref[:, h * head_w:(h + 1) * head_w], preferred_element_type=F32)
        k_ref[0, h, :, 0:MLA_NOPE] = rkv[:, 0:MLA_NOPE].astype(BF16)
        k_ref[0, h, :, MLA_NOPE:2 * MLA_NOPE] = k_rot
        v_ref[0, h] = rkv[:, MLA_NOPE:MLA_NOPE + MLA_V].astype(BF16)


def _mla_up(l, proj, cs_table, g_q, g_kv, wq_heads, wkv_heads):
    t = proj.shape[0]
    batch = t // SEQ
    n_s = SEQ // TM_UP
    qk_spec = pl.BlockSpec((1, MLA_HEADS, TM_UP, 2 * MLA_NOPE), lambda i: (i // n_s, 0, i % n_s, 0))
    return pl.pallas_call(
        _mla_up_kernel,
        name="mla_up",
        grid=(t // TM_UP,),
        in_specs=[
            pl.BlockSpec((TM_UP, MLA_Q_RANK), lambda i: (i, COL_CQ // MLA_Q_RANK)),
            pl.BlockSpec((TM_UP, MLA_KV_RANK), lambda i: (i, COL_CKV // MLA_KV_RANK)),
            pl.BlockSpec((TM_UP, 2 * MLA_ROPE), lambda i: (i, COL_KR // (2 * MLA_ROPE))),
            pl.BlockSpec((TM_UP, 2 * MLA_ROPE), lambda i: (i % n_s, 0)),
            _layer_vec(l, MLA_Q_RANK),
            _layer_vec(l, MLA_KV_RANK),
            pl.BlockSpec((None, MLA_Q_RANK, MLA_HEADS * 2 * MLA_NOPE), lambda i: (l, 0, 0)),
            pl.BlockSpec((None, MLA_KV_RANK, MLA_HEADS * (MLA_NOPE + MLA_V)), lambda i: (l, 0, 0)),
        ],
        out_specs=[
            qk_spec,
            qk_spec,
            pl.BlockSpec((1, MLA_HEADS, TM_UP, MLA_V), lambda i: (i // n_s, 0, i % n_s, 0)),
        ],
        out_shape=[
            jax.ShapeDtypeStruct((batch, MLA_HEADS, SEQ, 2 * MLA_NOPE), BF16),
            jax.ShapeDtypeStruct((batch, MLA_HEADS, SEQ, 2 * MLA_NOPE), BF16),
            jax.ShapeDtypeStruct((batch, MLA_HEADS, SEQ, MLA_V), BF16),
        ],
        compiler_params=_params("parallel"),
    )(proj, proj, proj, cs_table, g_q, g_kv, wq_heads, wkv_heads)


def _mla_attn_kernel(q_ref, k_ref, v_ref, o_ref):
    for c in range(SEQ // TQ_MLA):
        rows = slice(c * TQ_MLA, (c + 1) * TQ_MLA)
        for h in range(MLA_HEADS_PER_STEP):
            s = lax.dot_general(q_ref[0, h, rows, :], k_ref[0, h], (((1,), (1,)), ((), ())),
                                preferred_element_type=F32)
            m = jnp.max(s, axis=-1, keepdims=True)
            p = jnp.exp(s - m)
            l = jnp.sum(p, axis=-1, keepdims=True)
            o = jnp.dot(p.astype(BF16), v_ref[0, h], preferred_element_type=F32)
            o_ref[rows, h * MLA_V:(h + 1) * MLA_V] = (o / l).astype(o_ref.dtype)


def _mla_attn(q, k, v):
    batch = q.shape[0]
    qk_spec = pl.BlockSpec((1, MLA_HEADS_PER_STEP, SEQ, 2 * MLA_NOPE), lambda b, h: (b, h, 0, 0))
    return pl.pallas_call(
        _mla_attn_kernel,
        name="mla_attn",
        grid=(batch, MLA_HEADS // MLA_HEADS_PER_STEP),
        in_specs=[
            qk_spec,
            qk_spec,
            pl.BlockSpec((1, MLA_HEADS_PER_STEP, SEQ, MLA_V), lambda b, h: (b, h, 0, 0)),
        ],
        out_specs=pl.BlockSpec((SEQ, MLA_HEADS_PER_STEP * MLA_V), lambda b, h: (b, h)),
        out_shape=jax.ShapeDtypeStruct((batch * SEQ, MIX_MLA), BF16),
        compiler_params=_params("parallel", "parallel"),
    )(q, k, v)


def _na_pattern(pat, i, j):
    if pat == 0:
        return j <= NA_ROWS - 1, j - i + (NA_ROWS - 1)
    if pat == 1:
        return i <= j <= i + NA_ROWS - 1, j - i + (NA_ROWS - 1) - NA_QROWS
    return NA_KROWS - NA_ROWS <= j, j - i - 1


def _na_build_tables(rpb_base, rpb_ref, toep_ref, tab_ref):
    shape = (GRID_W, LANES)
    w_q = lax.broadcasted_iota(jnp.int32, shape, 0)
    lane = lax.broadcasted_iota(jnp.int32, shape, 1)
    w_k = lane & (GRID_W - 1)
    dc = w_k - w_q + (NA_COLS - 1)
    col_start = jnp.clip(w_q - NA_COLS // 2, 0, GRID_W - NA_COLS)
    col_valid = (w_k >= col_start) & (w_k < col_start + NA_COLS)
    left = lane < GRID_W
    neg = jnp.full(shape, MASK_VALUE, F32)

    def one_head(head, carry):
        for a in range(RPB_ROWS):
            acc = jnp.zeros(shape, F32)
            for bb in range(RPB_COLS):
                idx = rpb_base + head * (RPB_ROWS * RPB_COLS) + (a * RPB_COLS + bb)
                acc = jnp.where(dc == bb, rpb_ref[idx], acc)
            toep_ref[a] = acc
        for pat in range(3):
            for i in range(NA_QROWS):
                for jp in range(NA_KROWS // 2):
                    v_l, a_l = _na_pattern(pat, i, 2 * jp)
                    v_r, a_r = _na_pattern(pat, i, 2 * jp + 1)
                    if not (v_l or v_r):
                        tile = neg
                    else:
                        t_l = toep_ref[a_l] if v_l else neg
                        t_r = toep_ref[a_r] if v_r else neg
                        tile = jnp.where(col_valid, jnp.where(left, t_l, t_r), neg)
                    row0 = pl.multiple_of(head * NA_QB + i * GRID_W, GRID_W)
                    tab_ref[pat, pl.ds(row0, GRID_W), jp * LANES:(jp + 1) * LANES] = tile
        return carry

    lax.fori_loop(0, NA_HEADS, one_head, 0)


def _na_kernel(rpb_base, rpb_ref, q_ref, k_ref, v_ref, o_ref, toep_ref, tab_ref):
    step = pl.program_id(1)
    n_qb = GRID_ROWS // NA_QROWS

    @pl.when((pl.program_id(0) == 0) & (step == 0))
    def _():
        _na_build_tables(rpb_base, rpb_ref, toep_ref, tab_ref)

    lane = lax.broadcasted_iota(jnp.int32, (NA_QB, LANES), 1)
    left = lane < NA_HEAD_DIM
    for sub in range(NA_QB_PER_STEP):
        qb = step * NA_QB_PER_STEP + sub
        rows = slice(sub * NA_QB, (sub + 1) * NA_QB)
        start = pl.multiple_of(jnp.clip(qb - 1, 0, n_qb - NA_KROWS // NA_QROWS) * NA_QB, NA_QB)
        pat = jnp.where(qb == 0, 0, jnp.where(qb == n_qb - 1, 2, 1))
        for pair in range(NA_HEADS // 2):
            cols = slice(pair * LANES, (pair + 1) * LANES)
            k_win = k_ref[pl.ds(start, NA_KB), cols]
            v_win = v_ref[pl.ds(start, NA_KB), cols]
            q = q_ref[rows, cols] * (NA_HEAD_DIM ** -0.5)
            zero = jnp.zeros_like(q)
            q2 = jnp.concatenate([jnp.where(left, q, zero), jnp.where(left, zero, q)], axis=0)
            s = lax.dot_general(q2, k_win, (((1,), (1,)), ((), ())), preferred_element_type=F32)
            s = s + tab_ref[pat, pair * 2 * NA_QB:(pair + 1) * 2 * NA_QB, :]
            m = jnp.max(s, axis=-1, keepdims=True)
            p = jnp.exp(s - m)
            l = jnp.sum(p, axis=-1, keepdims=True)
            o2 = jnp.dot(p.astype(BF16), v_win, preferred_element_type=F32) / l
            o_ref[rows, cols] = jnp.where(left, o2[0:NA_QB], o2[NA_QB:2 * NA_QB]).astype(o_ref.dtype)


def _na_attn(l, proj, rpb_flat):
    t = proj.shape[0]
    batch = t // SEQ
    n_steps = GRID_ROWS // NA_QROWS // NA_QB_PER_STEP
    step_rows = NA_QB_PER_STEP * NA_QB
    return pl.pallas_call(
        functools.partial(_na_kernel, l * NA_HEADS * RPB_ROWS * RPB_COLS),
        name="na_attn",
        grid=(batch, n_steps),
        in_specs=[
            pl.BlockSpec(memory_space=pltpu.SMEM),
            pl.BlockSpec((step_rows, MIX_NA), lambda b, i: (b * n_steps + i, COL_QNA // MIX_NA)),
            pl.BlockSpec((SEQ, MIX_NA), lambda b, i: (b, COL_KNA // MIX_NA)),
            pl.BlockSpec((SEQ, MIX_NA), lambda b, i: (b, COL_VNA // MIX_NA)),
        ],
        out_specs=pl.BlockSpec((step_rows, MIX_NA), lambda b, i: (b * n_steps + i, 0)),
        out_shape=jax.ShapeDtypeStruct((t, MIX_NA), BF16),
        scratch_shapes=[
            pltpu.VMEM((RPB_ROWS, GRID_W, LANES), F32),
            pltpu.VMEM((3, NA_HEADS * NA_QB, NA_KB), F32),
        ],
        compiler_params=_params("arbitrary", "arbitrary"),
    )(rpb_flat, proj, proj, proj)


def _out_proj_kernel(yc_ref, ym_ref, yn_ref, gc_ref, gm_ref, gn_ref, w_ref, x_ref, gp_ref,
                     o_ref, mix_ref):
    mix_ref[:, 0:CONV_CH] = _rms(yc_ref[...].astype(F32), gc_ref[...]).astype(BF16)
    mix_ref[:, CONV_CH:CONV_CH + MIX_MLA] = _rms(ym_ref[...].astype(F32), gm_ref[...]).astype(BF16)
    mix_ref[:, CONV_CH + MIX_MLA:] = _rms(yn_ref[...].astype(F32), gn_ref[...]).astype(BF16)
    f = jnp.dot(mix_ref[...], w_ref[...], preferred_element_type=F32)
    o_ref[...] = x_ref[...] + _rms(f, gp_ref[...])


def _out_proj(l, y_conv, y_mla, y_na, g_c, g_m, g_n, w_o, x2d, g_post):
    t = x2d.shape[0]
    mix_width = CONV_CH + MIX_MLA + MIX_NA

    def rows(width):
        return pl.BlockSpec((TM_OUT, width), lambda i: (i, 0))

    def vec(width):
        return _layer_vec(l, width)

    return pl.pallas_call(
        _out_proj_kernel,
        name="out_proj",
        grid=(t // TM_OUT,),
        in_specs=[
            rows(CONV_CH), rows(MIX_MLA), rows(MIX_NA),
            vec(CONV_CH), vec(MIX_MLA), vec(MIX_NA),
            pl.BlockSpec((None, mix_width, D_MODEL), lambda i: (l, 0, 0)),
            rows(D_MODEL), vec(D_MODEL),
        ],
        out_specs=rows(D_MODEL),
        out_shape=jax.ShapeDtypeStruct((t, D_MODEL), F32),
        scratch_shapes=[pltpu.VMEM((TM_OUT, mix_width), BF16)],
        compiler_params=_params("parallel"),
    )(y_conv, y_mla, y_na, g_c, g_m, g_n, w_o, x2d, g_post)


def _ffn_kernel(x_ref, gpre_ref, wup_ref, wdn_ref, gpost_ref, o_ref, h_ref):
    k = pl.program_id(1)
    last = pl.num_programs(1) - 1

    @pl.when(k == 0)
    def _():
        h_ref[...] = _rms(x_ref[...], gpre_ref[...]).astype(BF16)

    part = None
    for c in range(TK_FFN // FFN_CHUNK):
        cols = slice(c * FFN_CHUNK, (c + 1) * FFN_CHUNK)
        a = jnp.maximum(jnp.dot(h_ref[...], wup_ref[:, cols], preferred_element_type=F32), 0.0)
        d = jnp.dot((a * a).astype(BF16), wdn_ref[cols, :], preferred_element_type=F32)
        part = d if part is None else part + d

    @pl.when(k == 0)
    def _():
        o_ref[...] = part

    @pl.when((k > 0) & (k < last))
    def _():
        o_ref[...] += part

    @pl.when(k == last)
    def _():
        o_ref[...] = x_ref[...] + _rms(o_ref[...] + part, gpost_ref[...])


def _ffn(l, x2d, g_pre, w_up, w_down, g_post):
    t = x2d.shape[0]
    vec = _layer_vec(l, D_MODEL)
    return pl.pallas_call(
        _ffn_kernel,
        name="ffn",
        grid=(t // TM_FFN, D_FF // TK_FFN),
        in_specs=[
            pl.BlockSpec((TM_FFN, D_MODEL), lambda i, k: (i, 0)),
            vec,
            pl.BlockSpec((None, D_MODEL, TK_FFN), lambda i, k: (l, 0, k)),
            pl.BlockSpec((None, TK_FFN, D_MODEL), lambda i, k: (l, k, 0)),
            vec,
        ],
        out_specs=pl.BlockSpec((TM_FFN, D_MODEL), lambda i, k: (i, 0)),
        out_shape=jax.ShapeDtypeStruct((t, D_MODEL), F32),
        scratch_shapes=[pltpu.VMEM((TM_FFN, D_MODEL), BF16)],
        compiler_params=pltpu.CompilerParams(dimension_semantics=("parallel", "arbitrary"),
                                             vmem_limit_bytes=FFN_VMEM_LIMIT_BYTES),
    )(x2d, g_pre, w_up, w_down, g_post)


def _rotate_half_cols(w):
    half = MLA_ROPE // 2
    return jnp.concatenate([-w[..., half:], w[..., :half]], axis=-1)


def _prep_w_in(w_in):
    sizes = (2 * CONV_CH, MLA_Q_RANK, MLA_KV_RANK, MLA_ROPE, MIX_NA, MIX_NA, MIX_NA)
    starts = [sum(sizes[:n]) for n in range(len(sizes))]
    w_in = w_in.astype(BF16)
    u_conv, c_q, c_kv, k_rope, q_na, k_na, v_na = (w_in[..., s:s + n] for s, n in zip(starts, sizes))
    pad = jnp.zeros(w_in.shape[:-1] + (IN_COLS_PAD - IN_COLS_EXT,), BF16)
    parts = [u_conv, c_q, q_na, k_na, v_na, c_kv, k_rope, _rotate_half_cols(k_rope), pad]
    return jnp.concatenate(parts, axis=-1)


def _prep_w_uq(w_uq):
    depth = w_uq.shape[0]
    w = w_uq.astype(BF16).reshape(depth, MLA_Q_RANK, MLA_HEADS, MLA_NOPE + MLA_ROPE)
    pe = w[..., MLA_NOPE:]
    w = jnp.concatenate([w[..., :MLA_NOPE], pe, _rotate_half_cols(pe)], axis=-1)
    return w.reshape(depth, MLA_Q_RANK, MLA_HEADS * 2 * MLA_NOPE)


def _rope_table():
    pos = jnp.arange(SEQ, dtype=F32)
    inv_freq = 1.0 / (ROPE_THETA ** (jnp.arange(0, MLA_ROPE, 2, dtype=F32) / MLA_ROPE))
    ang = pos[:, None] * inv_freq[None, :]
    cos, sin = jnp.cos(ang), jnp.sin(ang)
    return jnp.concatenate([cos, cos, sin, sin], axis=-1)


def _stack_rows(g):
    return g.reshape(g.shape[0], 1, g.shape[1])


def kernel(x, g_pre_mix, w_in, conv_w, conv_b, conv_ln_g, conv_ln_b, g_q_a, w_uq, g_kv_a, w_ukv,
           na_rpb, g_out_conv, g_out_mla, g_out_na, w_o, g_post_mix, g_pre_ffn, w_up, w_down,
           g_post_ffn):
    batch, seq, d_model = x.shape
    assert (seq, d_model) == (SEQ, D_MODEL)
    depth = w_in.shape[0]
    cs_table = _rope_table()
    w_in_ext = _prep_w_in(w_in)
    wq_heads = _prep_w_uq(w_uq)
    wkv_heads = w_ukv.astype(BF16)
    w_o_b, w_up_b, w_down_b = w_o.astype(BF16), w_up.astype(BF16), w_down.astype(BF16)
    rpb_flat = na_rpb.reshape(-1)
    (g_pre_mix, conv_b, conv_ln_g, conv_ln_b, g_q_a, g_kv_a, g_out_conv, g_out_mla, g_out_na,
     g_post_mix, g_pre_ffn, g_post_ffn) = map(_stack_rows, (
         g_pre_mix, conv_b, conv_ln_g, conv_ln_b, g_q_a, g_kv_a, g_out_conv, g_out_mla, g_out_na,
         g_post_mix, g_pre_ffn, g_post_ffn))
    x2d = x.reshape(batch * seq, d_model)
    for l in range(depth):
        proj = _in_proj(l, x2d, g_pre_mix, w_in_ext)
        y_conv = _conv(l, proj, conv_w, conv_b, conv_ln_g, conv_ln_b)
        q, k, v = _mla_up(l, proj, cs_table, g_q_a, g_kv_a, wq_heads, wkv_heads)
        y_mla = _mla_attn(q, k, v)
        y_na = _na_attn(l, proj, rpb_flat)
        x2d = _out_proj(l, y_conv, y_mla, y_na, g_out_conv, g_out_mla, g_out_na, w_o_b, x2d, g_post_mix)
        x2d = _ffn(l, x2d, g_pre_ffn, w_up_b, w_down_b, g_post_ffn)
    return x2d.reshape(batch, seq, d_model)
```
